```python
import jax, jax.numpy as jnp
from jax import lax
import numpy as np

D_MODEL = 1024
BATCH = 8
SEQ = 4096
DEPTH = 2
DEC_BATCH = 16
DEC_SEQ = 32
PAST_LEN = 1024

CHUNK = 64
EXPAND = 2
MIX_WIDTH = EXPAND * D_MODEL
N_A_LAYERS = DEPTH // 2
N_B_LAYERS = DEPTH - N_A_LAYERS
RWKV_HEAD = 64
RWKV_HEADS = MIX_WIDTH // RWKV_HEAD
DECAY_RANK = 64
ICLR_RANK = 64
SB_HEAD = 128
SB_HEADS = MIX_WIDTH // SB_HEAD
BLOCK_Q = 128
RMS_EPS = 1e-6
GN_EPS = 64e-5
L2_EPS = 1e-12

kernel_name = "yoco_rwkv7_stickbreaking_stream"


def rms_norm(x, g):
    xf = x.astype(jnp.float32)
    y = xf * lax.rsqrt(jnp.mean(xf * xf, axis=-1, keepdims=True) + RMS_EPS)
    return y.astype(x.dtype) * g


def ada_modulate(x, c, g, w_ada, b_ada):
    shift, scale, gate = jnp.split(c @ w_ada + b_ada, 3, axis=-1)
    h = rms_norm(x, g) * (1 + scale[:, None]) + shift[:, None]
    return h, gate[:, None]


def l2_normalize(x):
    xf = x.astype(jnp.float32)
    return (xf * lax.rsqrt(jnp.sum(xf * xf, axis=-1, keepdims=True) + L2_EPS)).astype(x.dtype)


def rwkv7_recurrence(r, decay, k, v, kk, a, s0):
    def step(s, inp):
        r_t, w_t, k_t, v_t, kk_t, a_t = inp
        s_kk = jnp.einsum('bhvk,bhk->bhv', s, kk_t)
        s = (s * w_t[:, :, None, :]
             - s_kk[..., None] * (kk_t * a_t)[:, :, None, :]
             + v_t[..., None] * k_t[:, :, None, :])
        return s, jnp.einsum('bhvk,bhk->bhv', s, r_t)
    xs = tuple(jnp.swapaxes(t.astype(jnp.float32), 0, 1) for t in (r, decay, k, v, kk, a))
    s_fin, ys = lax.scan(step, s0.astype(jnp.float32), xs)
    return jnp.swapaxes(ys, 0, 1).astype(v.dtype), s_fin.astype(s0.dtype)


def rwkv7_layer(x, c, shift_prev, s0, norm_g, ada_w, ada_b, w_in, mu_in, mu_w, mu_a,
                w0, w1, w2, a0, a1, a2, k_k, k_a, r_k, ln_g, ln_b, w_out):
    B, T, _ = x.shape
    h, gate = ada_modulate(x, c, norm_g, ada_w, ada_b)
    h_prev = jnp.concatenate([shift_prev[:, None].astype(h.dtype), h[:, :-1]], axis=1)
    dx = h_prev - h
    w_mu = (w_in.reshape(D_MODEL, 4, MIX_WIDTH) * mu_in.T[:, :, None]).reshape(D_MODEL, 4 * MIX_WIDTH)
    proj = jnp.concatenate([h, dx], axis=-1) @ jnp.concatenate([w_in, w_mu], axis=0)
    r, k, v, z = jnp.split(proj, 4, axis=-1)
    xw = h + dx * mu_w
    xa = h + dx * mu_a
    w_log = -jax.nn.softplus(-(w0 + jnp.tanh(xw @ w1) @ w2)) - 0.5
    decay = jnp.exp(-jnp.exp(w_log.astype(jnp.float32)))
    a = jax.nn.sigmoid(a0 + (xa @ a1) @ a2)
    heads = lambda t: t.reshape(B, T, RWKV_HEADS, RWKV_HEAD)
    r, k, v, a, decay = heads(r), heads(k), heads(v), heads(a), heads(decay)
    kk = l2_normalize(k * k_k.reshape(RWKV_HEADS, RWKV_HEAD))
    k = k * (1 + (a - 1) * k_a.reshape(RWKV_HEADS, RWKV_HEAD))
    y, s_new = rwkv7_recurrence(r, decay, k, v, kk, a, s0)
    yf = y.astype(jnp.float32)
    mean = jnp.mean(yf, axis=-1, keepdims=True)
    var = jnp.mean(jnp.square(yf - mean), axis=-1, keepdims=True)
    y = ((yf - mean) * lax.rsqrt(var + GN_EPS)).astype(y.dtype)
    y = y * ln_g.reshape(RWKV_HEADS, RWKV_HEAD) + ln_b.reshape(RWKV_HEADS, RWKV_HEAD)
    y = y + jnp.sum(r * k * r_k, axis=-1, keepdims=True) * v
    y = y.reshape(B, T, MIX_WIDTH) * jax.nn.silu(z)
    return x + gate * (y @ w_out), h[:, -1], s_new


def head_rms(x, g):
    xf = x.astype(jnp.float32)
    y = xf * lax.rsqrt(jnp.mean(xf * xf, axis=-1, keepdims=True) + RMS_EPS)
    return y.astype(x.dtype) * g


def shared_kv(x_mid, kv_norm_g, kv_w, k_gain):
    B, T, _ = x_mid.shape
    k, v = jnp.split(rms_norm(x_mid, kv_norm_g) @ kv_w, 2, axis=-1)
    k = head_rms(k.reshape(B, T, SB_HEADS, SB_HEAD), k_gain)
    return k, v.reshape(B, T, SB_HEADS, SB_HEAD)


def sb_block(q, k, v, q_start):
    Tq, Tk = q.shape[1], k.shape[1]
    z = jnp.einsum('bqhd,bkhd->bhqk', q, k).astype(jnp.float32) * (SB_HEAD ** -0.5)
    q_pos = q_start + jnp.arange(Tq)
    k_pos = jnp.arange(Tk)
    mask = k_pos[None, :] < q_pos[:, None]
    log_stay = jnp.where(mask, jax.nn.log_sigmoid(-z), 0.0)
    tail = lax.cumsum(log_stay, axis=3, reverse=True) - log_stay
    weights = jnp.where(mask, jnp.exp(jax.nn.log_sigmoid(z) + tail), 0.0)
    return jnp.einsum('bhqk,bkhd->bqhd', weights.astype(v.dtype), v)


def sb_sweep(q, k, v, offset):
    T = q.shape[1]
    outs = []
    for s in range(0, T, BLOCK_Q):
        e = min(s + BLOCK_Q, T)
        outs.append(sb_block(q[:, s:e], k[:, :offset + e], v[:, :offset + e], offset + s))
    return jnp.concatenate(outs, axis=1)


def sb_layer(x, c, k_all, v_all, offset, norm_g, ada_w, ada_b, w_in, q_gain, w_out):
    B, T, _ = x.shape
    h, gate = ada_modulate(x, c, norm_g, ada_w, ada_b)
    q, z = jnp.split(h @ w_in, 2, axis=-1)
    q = head_rms(q.reshape(B, T, SB_HEADS, SB_HEAD), q_gain)
    o = sb_sweep(q, k_all, v_all, offset).reshape(B, T, MIX_WIDTH)
    return x + gate * ((o * jax.nn.silu(z)) @ w_out)


def trunk(x, c, past_k, past_v, shift0, wkv0, p):
    new_shift, new_wkv = [], []
    k_new = v_new = k_all = v_all = None
    offset = 0 if past_k is None else past_k.shape[1]
    for layer in range(DEPTH):
        if layer < N_A_LAYERS:
            i = layer
            x, sh, s = rwkv7_layer(
                x, c, shift0[i], wkv0[i], p['a_norm_g'][i], p['a_ada_w'][i], p['a_ada_b'][i],
                p['a_w_in'][i], p['a_mu_in'][i], p['a_mu_w'][i], p['a_mu_a'][i],
                p['a_w0'][i], p['a_w1'][i], p['a_w2'][i], p['a_a0'][i], p['a_a1'][i], p['a_a2'][i],
                p['a_k_k'][i], p['a_k_a'][i], p['a_r_k'][i], p['a_ln_g'][i], p['a_ln_b'][i],
                p['a_w_out'][i])
            new_shift.append(sh)
            new_wkv.append(s)
            if layer == N_A_LAYERS - 1:
                k_new, v_new = shared_kv(x, p['kv_norm_g'], p['kv_w'], p['k_gain'])
                if past_k is None:
                    k_all, v_all = k_new, v_new
                else:
                    k_all = jnp.concatenate([past_k.astype(k_new.dtype), k_new], axis=1)
                    v_all = jnp.concatenate([past_v.astype(v_new.dtype), v_new], axis=1)
        else:
            j = layer - N_A_LAYERS
            x = sb_layer(x, c, k_all, v_all, offset, p['b_norm_g'][j], p['b_ada_w'][j],
                         p['b_ada_b'][j], p['b_w_in'][j], p['b_q_gain'][j], p['b_w_out'][j])
    return x, k_new, v_new, jnp.stack(new_wkv), jnp.stack(new_shift)


def setup_inputs(seed: int = 0) -> dict:
    key = jax.random.key(seed)
    ks = iter(jax.random.split(key, 48))
    f32 = jnp.float32
    nrm = lambda shape, s: jax.random.normal(next(ks), shape, f32) * s
    uni = lambda shape: jax.random.uniform(next(ks), shape, f32)
    D, E, NA, NB = D_MODEL, MIX_WIDTH, N_A_LAYERS, N_B_LAYERS
    return {
        'x_prompt': nrm((BATCH, SEQ, D), 1.0),
        'x_sample': nrm((DEC_BATCH, DEC_SEQ, D), 1.0),
        'cache_k': nrm((DEC_BATCH, PAST_LEN, SB_HEADS, SB_HEAD), 1.0),
        'cache_v': nrm((DEC_BATCH, PAST_LEN, SB_HEADS, SB_HEAD), 1.0),
        'state_wkv': nrm((NA, DEC_BATCH, RWKV_HEADS, RWKV_HEAD, RWKV_HEAD), RWKV_HEAD ** -0.5),
        'state_shift': nrm((NA, DEC_BATCH, D), 1.0),
        'c_prompt': nrm((BATCH, D), 1.0),
        'c_sample': nrm((DEC_BATCH, D), 1.0),
        'a_norm_g': 1.0 + nrm((NA, D), 0.05),
        'a_ada_w': nrm((NA, D, 3 * D), 0.5 * D ** -0.5),
        'a_ada_b': nrm((NA, 3 * D), 0.01),
        'a_w_in': nrm((NA, D, 4 * E), D ** -0.5),
        'a_mu_in': uni((NA, 4, D)),
        'a_mu_w': uni((NA, D)),
        'a_mu_a': uni((NA, D)),
        'a_w0': nrm((NA, E), 0.5),
        'a_w1': nrm((NA, D, DECAY_RANK), D ** -0.5),
        'a_w2': nrm((NA, DECAY_RANK, E), 0.5 * DECAY_RANK ** -0.5),
        'a_a0': nrm((NA, E), 0.5),
        'a_a1': nrm((NA, D, ICLR_RANK), D ** -0.5),
        'a_a2': nrm((NA, ICLR_RANK, E), 0.5 * ICLR_RANK ** -0.5),
        'a_k_k': 0.85 + nrm((NA, E), 0.05),
        'a_k_a': 1.0 + nrm((NA, E), 0.05),
        'a_r_k': nrm((NA, RWKV_HEADS, RWKV_HEAD), 0.1),
        'a_ln_g': 1.0 + nrm((NA, E), 0.05),
        'a_ln_b': nrm((NA, E), 0.01),
        'a_w_out': nrm((NA, E, D), E ** -0.5),
        'kv_norm_g': 1.0 + nrm((D,), 0.05),
        'kv_w': nrm((D, 2 * E), D ** -0.5),
        'k_gain': 1.0 + nrm((SB_HEAD,), 0.05),
        'b_norm_g': 1.0 + nrm((NB, D), 0.05),
        'b_ada_w': nrm((NB, D, 3 * D), 0.5 * D ** -0.5),
        'b_ada_b': nrm((NB, 3 * D), 0.01),
        'b_w_in': nrm((NB, D, 2 * E), D ** -0.5),
        'b_q_gain': 1.0 + nrm((NB, SB_HEAD), 0.05),
        'b_w_out': nrm((NB, E, D), E ** -0.5),
    }


def reference(x_prompt, x_sample, cache_k, cache_v, state_wkv, state_shift, c_prompt, c_sample,
              a_norm_g, a_ada_w, a_ada_b, a_w_in, a_mu_in, a_mu_w, a_mu_a, a_w0, a_w1, a_w2,
              a_a0, a_a1, a_a2, a_k_k, a_k_a, a_r_k, a_ln_g, a_ln_b, a_w_out,
              kv_norm_g, kv_w, k_gain,
              b_norm_g, b_ada_w, b_ada_b, b_w_in, b_q_gain, b_w_out):
    p = dict(a_norm_g=a_norm_g, a_ada_w=a_ada_w, a_ada_b=a_ada_b, a_w_in=a_w_in, a_mu_in=a_mu_in,
             a_mu_w=a_mu_w, a_mu_a=a_mu_a, a_w0=a_w0, a_w1=a_w1, a_w2=a_w2, a_a0=a_a0,
             a_a1=a_a1, a_a2=a_a2, a_k_k=a_k_k, a_k_a=a_k_a, a_r_k=a_r_k, a_ln_g=a_ln_g,
             a_ln_b=a_ln_b, a_w_out=a_w_out, kv_norm_g=kv_norm_g, kv_w=kv_w, k_gain=k_gain,
             b_norm_g=b_norm_g, b_ada_w=b_ada_w, b_ada_b=b_ada_b, b_w_in=b_w_in,
             b_q_gain=b_q_gain, b_w_out=b_w_out)
    bp = x_prompt.shape[0]
    shift0_p = jnp.zeros((N_A_LAYERS, bp, D_MODEL), x_prompt.dtype)
    wkv0_p = jnp.zeros((N_A_LAYERS, bp, RWKV_HEADS, RWKV_HEAD, RWKV_HEAD), x_prompt.dtype)
    y_prompt, k_prompt, v_prompt, wkv_prompt, shift_prompt = trunk(
        x_prompt, c_prompt, None, None, shift0_p, wkv0_p, p)
    y_sample, k_sample, v_sample, wkv_sample, shift_sample = trunk(
        x_sample, c_sample, cache_k, cache_v, state_shift, state_wkv, p)
    return (y_prompt, y_sample, k_prompt, v_prompt, wkv_prompt, shift_prompt,
            k_sample, v_sample, wkv_sample, shift_sample)
```

```python
import functools

import jax
import jax.numpy as jnp
from jax import lax
from jax.experimental import pallas as pl
from jax.experimental.pallas import tpu as pltpu

F32 = jnp.float32
BF16 = jnp.bfloat16

D_MODEL = 1024
MIX_WIDTH = 2048
RWKV_HEAD = 64
SB_HEAD = 128
SB_HEADS = MIX_WIDTH // SB_HEAD
LORA_RANK = 64
RMS_EPS = 1e-6
GN_EPS = 64e-5
L2_EPS = 1e-12
SB_SCALE = SB_HEAD ** -0.5

V7X_LANES = 128
V7X_MXU_WIDTH = 256
V7X_VMEM_BYTES = 64 * 1024 * 1024

GROUP_LANES = V7X_MXU_WIDTH
HEADS_PER_GROUP = GROUP_LANES // RWKV_HEAD
N_GROUPS = MIX_WIDTH // GROUP_LANES
KEY_BLOCK = V7X_LANES


def _vmem_limit(estimate_bytes):
    return int(min(V7X_VMEM_BYTES - 8 * 1024 * 1024, max(32 * 1024 * 1024, 2 * estimate_bytes)))


def _params(semantics, estimate_bytes):
    return pltpu.CompilerParams(dimension_semantics=semantics,
                                vmem_limit_bytes=_vmem_limit(estimate_bytes))


def _split_hi_lo(x):
    hi = x.astype(BF16)
    lo = (x - hi.astype(F32)).astype(BF16)
    return jnp.concatenate([hi, lo], axis=-1)


def _seg_sum(x, seg_w):
    m, c = x.shape
    tiles = c // GROUP_LANES
    stacked = jnp.concatenate(
        [_split_hi_lo(x[:, j * GROUP_LANES:(j + 1) * GROUP_LANES]) for j in range(tiles)], axis=0)
    res = jnp.dot(stacked, seg_w, preferred_element_type=F32)
    return jnp.concatenate([res[j * m:(j + 1) * m] for j in range(tiles)], axis=1)


def _seg_weights(seg):
    j = jnp.arange(2 * GROUP_LANES) % GROUP_LANES
    l = jnp.arange(GROUP_LANES)
    return ((j[:, None] // seg) == (l[None, :] // seg)).astype(BF16)


def _log_sigmoid(x):
    return jnp.minimum(x, 0.0) - jnp.log(1.0 + jnp.exp(-jnp.abs(x)))


def _sigmoid(x):
    return 1.0 / (1.0 + jnp.exp(-x))


def _rms_mod(x, g, scale, shift):
    ms = jnp.mean(x * x, axis=-1, keepdims=True)
    return (x * lax.rsqrt(ms + RMS_EPS)) * g * (1.0 + scale) + shift


def _ada_kernel(c_ref, w_ref, b_ref, o_ref):
    o_ref[...] = jnp.dot(c_ref[...], w_ref[...], precision=lax.Precision.HIGHEST,
                         preferred_element_type=F32) + b_ref[...]


def _ada(c_all, w, b):
    m = c_all.shape[0]
    layers = w.shape[0]
    return pl.pallas_call(
        _ada_kernel,
        grid=(layers, 3),
        in_specs=[pl.BlockSpec((m, D_MODEL), lambda l, n: (0, 0)),
                  pl.BlockSpec((None, D_MODEL, D_MODEL), lambda l, n: (l, 0, n)),
                  pl.BlockSpec((None, 1, D_MODEL), lambda l, n: (l, 0, n))],
        out_specs=pl.BlockSpec((None, m, D_MODEL), lambda l, n: (l, 0, n)),
        out_shape=jax.ShapeDtypeStruct((layers, m, 3 * D_MODEL), F32),
        compiler_params=_params(("arbitrary", "arbitrary"), 2 * 4 * D_MODEL * D_MODEL),
        name="ada",
    )(c_all, w, b)


def _pre_kernel(x_ref, sc_ref, sh_ref, g_ref, sp_ref, mu_ref, muw_ref, mua_ref, w1_ref, a1_ref,
                wr_ref, wk_ref, wv_ref, wz_ref, w2_ref, a2_ref, w0_ref, a0_ref, kk_ref, ka_ref,
                segw_ref,
                r_out, kkn_out, dec_out, nb_out, km_out, v_out, z_out, shift_out,
                xs_scr, lw_scr, la_scr, carry_scr):
    t = pl.program_id(1)
    n = pl.program_id(2)
    last_t = pl.num_programs(1) - 1

    @pl.when(n == 0)
    def _():
        @pl.when(t == 0)
        def _():
            carry_scr[...] = sp_ref[...]

        h = _rms_mod(x_ref[...], g_ref[...], sc_ref[...], sh_ref[...])
        tm = h.shape[0]
        row = lax.broadcasted_iota(jnp.int32, h.shape, 0)
        h_prev = jnp.where(row == 0, carry_scr[...], pltpu.roll(h, 1, axis=0))
        dx = h_prev - h
        for p in range(4):
            xs_scr[p] = (h + mu_ref[p:p + 1, :] * dx).astype(BF16)
        xw = (h + dx * muw_ref[...]).astype(BF16)
        lw_scr[...] = jnp.tanh(jnp.dot(xw, w1_ref[...], preferred_element_type=F32))
        xa = (h + dx * mua_ref[...]).astype(BF16)
        la_scr[...] = jnp.dot(xa, a1_ref[...], preferred_element_type=F32)
        carry_scr[...] = h[tm - 1:tm, :]

        @pl.when(t == last_t)
        def _():
            shift_out[...] = h[tm - 1:tm, :]

    r = jnp.dot(xs_scr[0], wr_ref[...], preferred_element_type=F32)
    k = jnp.dot(xs_scr[1], wk_ref[...], preferred_element_type=F32)
    v = jnp.dot(xs_scr[2], wv_ref[...], preferred_element_type=F32)
    z = jnp.dot(xs_scr[3], wz_ref[...], preferred_element_type=F32)
    wl = w0_ref[...] + jnp.dot(lw_scr[...].astype(BF16), w2_ref[...], preferred_element_type=F32)
    dec = jnp.exp(-jnp.exp(_log_sigmoid(wl) - 0.5))
    a = _sigmoid(a0_ref[...] + jnp.dot(la_scr[...].astype(BF16), a2_ref[...],
                                       preferred_element_type=F32))
    kk = k * kk_ref[...]
    kkn = kk * lax.rsqrt(_seg_sum(kk * kk, segw_ref[...]) + L2_EPS)
    r_out[...] = r
    kkn_out[...] = kkn
    dec_out[...] = dec
    nb_out[...] = -(kkn * a)
    km_out[...] = k * (1.0 + (a - 1.0) * ka_ref[...])
    v_out[...] = v
    z_out[...] = z


def _rwkv_pre(x, scale, shift, shift_prev, p, seg64, tm, tn=512):
    b, t, _ = x.shape
    nt = MIX_WIDTH // tn
    row = lambda w: pl.BlockSpec((1, w), lambda i, j, n: (0, 0))
    per_b = pl.BlockSpec((None, 1, D_MODEL), lambda i, j, n: (i, 0, 0))
    col = lambda rows: pl.BlockSpec((rows, tn), lambda i, j, n: (0, n))
    w_in_part = lambda part: pl.BlockSpec((D_MODEL, tn), lambda i, j, n: (0, part * nt + n))
    out_blk = pl.BlockSpec((None, tm, tn), lambda i, j, n: (i, j, n))
    act = jax.ShapeDtypeStruct((b, t, MIX_WIDTH), F32)
    est = (2 * tm * D_MODEL * 4 + 4 * tm * D_MODEL * 2 + 2 * 4 * D_MODEL * tn * 2
           + 2 * 7 * tm * tn * 4 + 8 * tm * tn * 4)
    return pl.pallas_call(
        _pre_kernel,
        grid=(b, t // tm, nt),
        in_specs=[pl.BlockSpec((None, tm, D_MODEL), lambda i, j, n: (i, j, 0)),
                  per_b, per_b, row(D_MODEL), per_b,
                  pl.BlockSpec((4, D_MODEL), lambda i, j, n: (0, 0)), row(D_MODEL), row(D_MODEL),
                  pl.BlockSpec((D_MODEL, LORA_RANK), lambda i, j, n: (0, 0)),
                  pl.BlockSpec((D_MODEL, LORA_RANK), lambda i, j, n: (0, 0)),
                  w_in_part(0), w_in_part(1), w_in_part(2), w_in_part(3),
                  col(LORA_RANK), col(LORA_RANK), col(1), col(1), col(1), col(1),
                  pl.BlockSpec(seg64.shape, lambda i, j, n: (0, 0))],
        out_specs=[out_blk] * 7 + [per_b],
        out_shape=[act] * 7 + [jax.ShapeDtypeStruct((b, 1, D_MODEL), F32)],
        scratch_shapes=[pltpu.VMEM((4, tm, D_MODEL), BF16),
                        pltpu.VMEM((tm, LORA_RANK), F32),
                        pltpu.VMEM((tm, LORA_RANK), F32),
                        pltpu.VMEM((1, D_MODEL), F32)],
        compiler_params=_params(("arbitrary", "arbitrary", "arbitrary"), est),
        name="rwkv_pre",
    )(x, scale, shift, p['norm_g'], shift_prev, p['mu_in'], p['mu_w'], p['mu_a'], p['w1'], p['a1'],
      p['w_in'], p['w_in'], p['w_in'], p['w_in'], p['w2'], p['a2'], p['w0'], p['a0'],
      p['k_k'], p['k_a'], seg64)


def _rec_kernel(kk_ref, dec_ref, nb_ref, km_ref, v_ref, r_ref, s0_ref, segw_ref, eye_ref,
                y_out, s_out, s_scr, yt_scr, *, tb):
    t = pl.program_id(1)

    @pl.when(t == 0)
    def _():
        s_scr[...] = s0_ref[...]

    yt_scr[...] = jnp.zeros_like(yt_scr)
    seg_w = segw_ref[...]
    eye = eye_ref[...]
    token_lane = lax.broadcasted_iota(jnp.int32, (RWKV_HEAD, GROUP_LANES), 1) % RWKV_HEAD

    def token_step(i, carry):
        grp = lambda ref, g: ref[pl.ds(i, 1), g * GROUP_LANES:(g + 1) * GROUP_LANES]
        lhs = [_split_hi_lo(s_scr[g] * grp(kk_ref, g)) for g in range(N_GROUPS)]
        lhs += [_split_hi_lo(eye * grp(v_ref, g)) for g in range(N_GROUPS)]
        red = jnp.dot(jnp.concatenate(lhs, axis=0), seg_w, preferred_element_type=F32)
        lhs = []
        for g in range(N_GROUPS):
            s_kk = red[g * RWKV_HEAD:(g + 1) * RWKV_HEAD]
            v_col = red[(N_GROUPS + g) * RWKV_HEAD:(N_GROUPS + g + 1) * RWKV_HEAD]
            s_new = s_scr[g] * grp(dec_ref, g) + s_kk * grp(nb_ref, g) + v_col * grp(km_ref, g)
            s_scr[g] = s_new
            lhs.append(_split_hi_lo(s_new * grp(r_ref, g)))
        y_b = jnp.dot(jnp.concatenate(lhs, axis=0), seg_w, preferred_element_type=F32)
        hit = token_lane == i
        for g in range(N_GROUPS):
            yt_scr[g] = jnp.where(hit, y_b[g * RWKV_HEAD:(g + 1) * RWKV_HEAD], yt_scr[g])
        return carry

    lax.fori_loop(0, tb, token_step, 0)

    for g in range(N_GROUPS):
        yt = yt_scr[g].T
        y_out[:, g * GROUP_LANES:(g + 1) * GROUP_LANES] = jnp.concatenate(
            [yt[h * RWKV_HEAD:h * RWKV_HEAD + tb] for h in range(HEADS_PER_GROUP)], axis=1)

    @pl.when(t == pl.num_programs(1) - 1)
    def _():
        s_out[...] = s_scr[...]


def _rwkv_recurrence(kkn, dec, nb, km, v, r, s0, seg64, eye, tb):
    b, t, _ = kkn.shape
    act_blk = pl.BlockSpec((None, tb, MIX_WIDTH), lambda i, j: (i, j, 0))
    state_blk = pl.BlockSpec((None, N_GROUPS, RWKV_HEAD, GROUP_LANES), lambda i, j: (i, 0, 0, 0))
    const = lambda a: pl.BlockSpec(a.shape, lambda i, j: (0, 0))
    state_bytes = N_GROUPS * RWKV_HEAD * GROUP_LANES * 4
    est = 2 * 7 * tb * MIX_WIDTH * 4 + 6 * state_bytes
    return pl.pallas_call(
        functools.partial(_rec_kernel, tb=tb),
        grid=(b, t // tb),
        in_specs=[act_blk] * 6 + [state_blk, const(seg64), const(eye)],
        out_specs=[act_blk, state_blk],
        out_shape=[jax.ShapeDtypeStruct((b, t, MIX_WIDTH), F32),
                   jax.ShapeDtypeStruct(s0.shape, F32)],
        scratch_shapes=[pltpu.VMEM((N_GROUPS, RWKV_HEAD, GROUP_LANES), F32),
                        pltpu.VMEM((N_GROUPS, RWKV_HEAD, GROUP_LANES), F32)],
        compiler_params=_params(("arbitrary", "arbitrary"), est),
        name="rwkv_recurrence",
    )(kkn, dec, nb, km, v, r, s0, seg64, eye)


def _post_kernel(y_ref, r_ref, km_ref, v_ref, z_ref, x_ref, gate_ref, lng_ref, lnb_ref, rk_ref,
                 wout_ref, segw_ref, o_ref):
    seg_w = segw_ref[...]
    inv = 1.0 / RWKV_HEAD
    y = y_ref[...]
    d = y - _seg_sum(y, seg_w) * inv
    var = _seg_sum(d * d, seg_w) * inv
    yn = d * lax.rsqrt(var + GN_EPS) * lng_ref[...] + lnb_ref[...]
    bonus = _seg_sum(r_ref[...] * km_ref[...] * rk_ref[...], seg_w) * v_ref[...]
    z = z_ref[...]
    gated = ((yn + bonus) * (z * _sigmoid(z))).astype(BF16)
    o_ref[...] = x_ref[...] + gate_ref[...] * jnp.dot(gated, wout_ref[...],
                                                      preferred_element_type=F32)


def _rwkv_post(y, r, km, v, z, x, gate, p, seg64, tm):
    b, t, _ = x.shape
    act_blk = pl.BlockSpec((None, tm, MIX_WIDTH), lambda i, j: (i, j, 0))
    x_blk = pl.BlockSpec((None, tm, D_MODEL), lambda i, j: (i, j, 0))
    row = pl.BlockSpec((1, MIX_WIDTH), lambda i, j: (0, 0))
    est = 2 * 5 * tm * MIX_WIDTH * 4 + 2 * MIX_WIDTH * D_MODEL * 2 + 12 * tm * MIX_WIDTH * 4
    return pl.pallas_call(
        _post_kernel,
        grid=(b, t // tm),
        in_specs=[act_blk] * 5 + [x_blk, pl.BlockSpec((None, 1, D_MODEL), lambda i, j: (i, 0, 0)),
                                  row, row, row,
                                  pl.BlockSpec((MIX_WIDTH, D_MODEL), lambda i, j: (0, 0)),
                                  pl.BlockSpec(seg64.shape, lambda i, j: (0, 0))],
        out_specs=x_blk,
        out_shape=jax.ShapeDtypeStruct(x.shape, F32),
        compiler_params=_params(("arbitrary", "arbitrary"), est),
        name="rwkv_post",
    )(y, r, km, v, z, x, gate, p['ln_g'], p['ln_b'], p['r_k'], p['w_out'], seg64)


def _norm_proj_kernel(x_ref, sc_ref, sh_ref, g_ref, wa_ref, wb_ref, gain_ref, segw_ref,
                      a_out, b_out, lhs_scr):
    @pl.when(pl.program_id(2) == 0)
    def _():
        lhs_scr[...] = _rms_mod(x_ref[...], g_ref[...], sc_ref[...], sh_ref[...]).astype(BF16)

    lhs = lhs_scr[...]
    a = jnp.dot(lhs, wa_ref[...], preferred_element_type=F32)
    ms = _seg_sum(a * a, segw_ref[...]) * (1.0 / SB_HEAD)
    a_out[...] = a * lax.rsqrt(ms + RMS_EPS) * gain_ref[...]
    b_out[...] = jnp.dot(lhs, wb_ref[...], preferred_element_type=F32)


def _norm_proj(x, scale, shift, g, w, gain_row, seg128, tm, tn=512):
    b, t, _ = x.shape
    nt = MIX_WIDTH // tn
    per_b = pl.BlockSpec((None, 1, D_MODEL), lambda i, j, n: (i, 0, 0))
    out_blk = pl.BlockSpec((None, tm, tn), lambda i, j, n: (i, j, n))
    act = jax.ShapeDtypeStruct((b, t, MIX_WIDTH), F32)
    est = 2 * tm * D_MODEL * 4 + tm * D_MODEL * 2 + 4 * D_MODEL * tn * 2 + 10 * tm * tn * 4
    return pl.pallas_call(
        _norm_proj_kernel,
        grid=(b, t // tm, nt),
        in_specs=[pl.BlockSpec((None, tm, D_MODEL), lambda i, j, n: (i, j, 0)), per_b, per_b,
                  pl.BlockSpec((1, D_MODEL), lambda i, j, n: (0, 0)),
                  pl.BlockSpec((D_MODEL, tn), lambda i, j, n: (0, n)),
                  pl.BlockSpec((D_MODEL, tn), lambda i, j, n: (0, nt + n)),
                  pl.BlockSpec((1, tn), lambda i, j, n: (0, 0)),
                  pl.BlockSpec(seg128.shape, lambda i, j, n: (0, 0))],
        out_specs=[out_blk, out_blk],
        out_shape=[act, act],
        scratch_shapes=[pltpu.VMEM((tm, D_MODEL), BF16)],
        compiler_params=_params(("arbitrary", "arbitrary", "arbitrary"), est),
        name="norm_proj",
    )(x, scale, shift, g, w, w, gain_row, seg128)


def _attn_kernel(q_ref, k_ref, v_ref, wat_ref, o_ref, *, tq, offset):
    i = pl.program_id(2)
    q = q_ref[...].astype(BF16)
    q_pos = offset + i * tq + lax.broadcasted_iota(jnp.int32, (tq, KEY_BLOCK), 0)
    top = (offset + (i + 1) * tq - 1) // KEY_BLOCK
    wat = wat_ref[...]

    def block(kb, carry, acc, masked):
        start = pl.multiple_of(kb * KEY_BLOCK, KEY_BLOCK)
        k_blk = k_ref[pl.ds(start, KEY_BLOCK), :].astype(BF16)
        v_blk = v_ref[pl.ds(start, KEY_BLOCK), :].astype(BF16)
        z = lax.dot_general(q, k_blk, (((1,), (1,)), ((), ())),
                            preferred_element_type=F32) * SB_SCALE
        log_stay = _log_sigmoid(-z)
        if masked:
            valid = (start + lax.broadcasted_iota(jnp.int32, (tq, KEY_BLOCK), 1)) < q_pos
            log_stay = jnp.where(valid, log_stay, 0.0)
        sums = jnp.dot(_split_hi_lo(log_stay), wat, preferred_element_type=F32)
        tail = sums[:, :KEY_BLOCK] + carry
        weights = jnp.exp(log_stay + z + tail)
        if masked:
            weights = jnp.where(valid, weights, 0.0)
        acc = acc + jnp.dot(weights.astype(BF16), v_blk, preferred_element_type=F32)
        return carry + sums[:, KEY_BLOCK:], acc

    zeros = jnp.zeros((tq, KEY_BLOCK), F32)
    carry, acc = block(top, zeros, zeros, True)
    carry, acc = lax.fori_loop(0, top, lambda j, ca: block(top - 1 - j, ca[0], ca[1], False),
                               (carry, acc))
    o_ref[...] = acc


def _attn_weights():
    j = jnp.arange(2 * KEY_BLOCK) % KEY_BLOCK
    s = jnp.arange(KEY_BLOCK)
    suffix = (j[:, None] > s[None, :]).astype(BF16)
    return jnp.concatenate([suffix, jnp.ones((2 * KEY_BLOCK, KEY_BLOCK), BF16)], axis=1)


def _sb_attention(q, k_all, v_all, offset, wat, tq):
    b, t, _ = q.shape
    tk = k_all.shape[1]
    assert offset % KEY_BLOCK == 0 and KEY_BLOCK % tq == 0 and tk % KEY_BLOCK == 0
    assert tk >= offset + t
    q_blk = pl.BlockSpec((None, tq, SB_HEAD), lambda i, h, j: (i, j, h))
    kv_blk = pl.BlockSpec((None, tk, SB_HEAD), lambda i, h, j: (i, 0, h))
    est = 2 * 2 * tk * SB_HEAD * 4 + 64 * tq * SB_HEAD * 4
    return pl.pallas_call(
        functools.partial(_attn_kernel, tq=tq, offset=offset),
        grid=(b, SB_HEADS, t // tq),
        in_specs=[q_blk, kv_blk, kv_blk, pl.BlockSpec(wat.shape, lambda i, h, j: (0, 0))],
        out_specs=q_blk,
        out_shape=jax.ShapeDtypeStruct(q.shape, F32),
        compiler_params=_params(("arbitrary", "arbitrary", "arbitrary"), est),
        name="sb_attention",
    )(q, k_all, v_all, wat)


def _gated_out_kernel(o_ref, z_ref, x_ref, gate_ref, w_ref, out_ref):
    z = z_ref[...]
    lhs = (o_ref[...] * (z * _sigmoid(z))).astype(BF16)
    out_ref[...] = x_ref[...] + gate_ref[...] * jnp.dot(lhs, w_ref[...],
                                                        preferred_element_type=F32)


def _gated_out(o, z, x, gate, w, tm):
    b, t, _ = x.shape
    act_blk = pl.BlockSpec((None, tm, MIX_WIDTH), lambda i, j: (i, j, 0))
    x_blk = pl.BlockSpec((None, tm, D_MODEL), lambda i, j: (i, j, 0))
    est = 2 * 2 * tm * MIX_WIDTH * 4 + 2 * MIX_WIDTH * D_MODEL * 2 + 4 * tm * MIX_WIDTH * 4
    return pl.pallas_call(
        _gated_out_kernel,
        grid=(b, t // tm),
        in_specs=[act_blk, act_blk, x_blk,
                  pl.BlockSpec((None, 1, D_MODEL), lambda i, j: (i, 0, 0)),
                  pl.BlockSpec((MIX_WIDTH, D_MODEL), lambda i, j: (0, 0))],
        out_specs=x_blk,
        out_shape=jax.ShapeDtypeStruct(x.shape, F32),
        compiler_params=_params(("arbitrary", "arbitrary"), est),
        name="gated_out",
    )(o, z, x, gate, w)


def _state_to_groups(s):
    b = s.shape[0]
    s = s.reshape(b, N_GROUPS, HEADS_PER_GROUP, RWKV_HEAD, RWKV_HEAD)
    return s.transpose(0, 1, 3, 2, 4).reshape(b, N_GROUPS, RWKV_HEAD, GROUP_LANES)


def _groups_to_state(s):
    b = s.shape[0]
    s = s.reshape(b, N_GROUPS, RWKV_HEAD, HEADS_PER_GROUP, RWKV_HEAD)
    return s.transpose(0, 1, 3, 2, 4).reshape(b, N_GROUPS * HEADS_PER_GROUP, RWKV_HEAD, RWKV_HEAD)


def _trunk(x, ada_a, ada_b, past_k, past_v, shift0, wkv0, pa, pb, consts):
    b, t, _ = x.shape
    seg64, seg128, eye, wat = consts
    split3 = lambda a: [v[:, None, :] for v in jnp.split(a, 3, axis=-1)]
    shift_a, scale_a, gate_a = split3(ada_a)
    shift_b, scale_b, gate_b = split3(ada_b)

    r, kkn, dec, nb, km, v, z, shift_new = _rwkv_pre(
        x, scale_a, shift_a, shift0[:, None, :], pa, seg64, tm=min(t, 512))
    y, s_new = _rwkv_recurrence(kkn, dec, nb, km, v, r, _state_to_groups(wkv0), seg64, eye,
                                tb=min(t, RWKV_HEAD))
    x_mid = _rwkv_post(y, r, km, v, z, x, gate_a, pa, seg64, tm=min(t, 128))

    zeros = jnp.zeros((b, 1, D_MODEL), F32)
    k_new, v_new = _norm_proj(x_mid, zeros, zeros, pb['kv_norm_g'], pb['kv_w'], pb['k_gain'],
                              seg128, tm=min(t, 512))
    q, z1 = _norm_proj(x_mid, scale_b, shift_b, pb['norm_g'], pb['w_in'], pb['q_gain'],
                       seg128, tm=min(t, 512))
    if past_k is None:
        offset, k_all, v_all = 0, k_new, v_new
    else:
        offset = past_k.shape[1]
        pad = (-(offset + t)) % KEY_BLOCK
        flat = lambda c: c.reshape(b, offset, MIX_WIDTH)
        tail = jnp.zeros((b, pad, MIX_WIDTH), F32)
        k_all = jnp.concatenate([flat(past_k), k_new, tail], axis=1)
        v_all = jnp.concatenate([flat(past_v), v_new, tail], axis=1)
    o = _sb_attention(q, k_all, v_all, offset, wat, tq=min(t, KEY_BLOCK))
    x_out = _gated_out(o, z1, x_mid, gate_b, pb['w_out'], tm=min(t, 256))

    heads = lambda a: a.reshape(b, t, SB_HEADS, SB_HEAD)
    return (x_out, heads(k_new), heads(v_new), _groups_to_state(s_new)[None],
            shift_new.reshape(1, b, D_MODEL))


def kernel(x_prompt, x_sample, cache_k, cache_v, state_wkv, state_shift, c_prompt, c_sample, a_norm_g, a_ada_w, a_ada_b, a_w_in, a_mu_in, a_mu_w, a_mu_a, a_w0, a_w1, a_w2, a_a0, a_a1, a_a2, a_k_k, a_k_a, a_r_k, a_ln_g, a_ln_b, a_w_out, kv_norm_g, kv_w, k_gain, b_norm_g, b_ada_w, b_ada_b, b_w_in, b_q_gain, b_w_out):
    bp = x_prompt.shape[0]
    row = lambda a: a.reshape(1, -1)
    pa = dict(norm_g=a_norm_g, mu_in=a_mu_in[0], mu_w=a_mu_w, mu_a=a_mu_a,
              w_in=a_w_in[0].astype(BF16), w1=a_w1[0].astype(BF16), a1=a_a1[0].astype(BF16),
              w2=a_w2[0].astype(BF16), a2=a_a2[0].astype(BF16), w0=a_w0, a0=a_a0,
              k_k=a_k_k, k_a=a_k_a, r_k=row(a_r_k[0]), ln_g=a_ln_g, ln_b=a_ln_b,
              w_out=a_w_out[0].astype(BF16))
    heads_per_tile = 512 // SB_HEAD
    pb = dict(kv_norm_g=row(kv_norm_g), kv_w=kv_w.astype(BF16),
              k_gain=jnp.tile(row(k_gain), (1, heads_per_tile)),
              norm_g=b_norm_g, w_in=b_w_in[0].astype(BF16),
              q_gain=jnp.tile(b_q_gain, (1, heads_per_tile)),
              w_out=b_w_out[0].astype(BF16))
    eye = (jnp.arange(GROUP_LANES)[None, :] % RWKV_HEAD
           == jnp.arange(RWKV_HEAD)[:, None]).astype(F32)
    consts = (_seg_weights(RWKV_HEAD), _seg_weights(SB_HEAD), eye, _attn_weights())

    ada = _ada(jnp.concatenate([c_prompt, c_sample], axis=0),
               jnp.concatenate([a_ada_w, b_ada_w], axis=0),
               jnp.concatenate([a_ada_b, b_ada_b], axis=0)[:, None, :])
    zero_shift = jnp.zeros((bp, D_MODEL), F32)
    zero_wkv = jnp.zeros((bp,) + state_wkv.shape[2:], F32)
    y_p, k_p, v_p, wkv_p, shift_p = _trunk(
        x_prompt, ada[0, :bp], ada[1, :bp], None, None, zero_shift, zero_wkv, pa, pb, consts)
    y_s, k_s, v_s, wkv_s, shift_s = _trunk(
        x_sample, ada[0, bp:], ada[1, bp:], cache_k, cache_v, state_shift[0], state_wkv[0],
        pa, pb, consts)
    return (y_p, y_s, k_p, v_p, wkv_p, shift_p, k_s, v_s, wkv_s, shift_s)
```

```python
import functools

import jax
import jax.numpy as jnp
from jax import lax
from jax.experimental import pallas as pl
from jax.experimental.pallas import tpu as pltpu

F32 = jnp.float32
BF16 = jnp.bfloat16

D_MODEL = 1024
MIX_WIDTH = 2048
RWKV_HEAD = 64
RWKV_HEADS = MIX_WIDTH // RWKV_HEAD
SB_HEAD = 128
SB_HEADS = MIX_WIDTH // SB_HEAD
LORA_RANK = 64
RMS_EPS = 1e-6
GN_EPS = 64e-5
L2_EPS = 1e-12
SB_SCALE = SB_HEAD ** -0.5
LOG2_E = 1.4426950408889634

V7X_LANES = 128
V7X_SUBLANES = 8
V7X_MXU_WIDTH = 256
V7X_VMEM_BYTES = 64 * 1024 * 1024

SEG_TILE = V7X_MXU_WIDTH
PAIR_ROWS = V7X_LANES
CHUNK_LANES = 2 * PAIR_ROWS
KEY_STEP = V7X_MXU_WIDTH


def _vmem_limit(estimate_bytes):
    return int(min(V7X_VMEM_BYTES - 8 * 1024 * 1024, max(32 * 1024 * 1024, 2 * estimate_bytes)))


def _params(semantics, estimate_bytes):
    return pltpu.CompilerParams(dimension_semantics=semantics,
                                vmem_limit_bytes=_vmem_limit(estimate_bytes))


def _split_hi_lo(x):
    hi = x.astype(BF16)
    lo = (x - hi.astype(F32)).astype(BF16)
    return jnp.concatenate([hi, lo], axis=-1)


def _seg_sum(x, seg_w):
    m, c = x.shape
    tiles = c // SEG_TILE
    stacked = jnp.concatenate(
        [_split_hi_lo(x[:, j * SEG_TILE:(j + 1) * SEG_TILE]) for j in range(tiles)], axis=0)
    res = jnp.dot(stacked, seg_w, preferred_element_type=F32)
    return jnp.concatenate([res[j * m:(j + 1) * m] for j in range(tiles)], axis=1)


def _seg_weights(seg):
    j = jnp.arange(2 * SEG_TILE) % SEG_TILE
    l = jnp.arange(SEG_TILE)
    return ((j[:, None] // seg) == (l[None, :] // seg)).astype(BF16)


def _log_sigmoid(x):
    return jnp.minimum(x, 0.0) - jnp.log(1.0 + jnp.exp(-jnp.abs(x)))


def _sigmoid(x):
    return 1.0 / (1.0 + jnp.exp(-x))


def _rms_mod(x, g, scale, shift):
    ms = jnp.mean(x * x, axis=-1, keepdims=True)
    return (x * lax.rsqrt(ms + RMS_EPS)) * g * (1.0 + scale) + shift


def _ada_kernel(c_ref, w_ref, b_ref, o_ref):
    o_ref[...] = jnp.dot(c_ref[...], w_ref[...], precision=lax.Precision.HIGHEST,
                         preferred_element_type=F32) + b_ref[...]


def _ada(c_all, w, b):
    m = c_all.shape[0]
    layers = w.shape[0]
    return pl.pallas_call(
        _ada_kernel,
        grid=(layers, 3),
        in_specs=[pl.BlockSpec((m, D_MODEL), lambda l, n: (0, 0)),
                  pl.BlockSpec((None, D_MODEL, D_MODEL), lambda l, n: (l, 0, n)),
                  pl.BlockSpec((None, 1, D_MODEL), lambda l, n: (l, 0, n))],
        out_specs=pl.BlockSpec((None, m, D_MODEL), lambda l, n: (l, 0, n)),
        out_shape=jax.ShapeDtypeStruct((layers, m, 3 * D_MODEL), F32),
        compiler_params=_params(("arbitrary", "arbitrary"), 2 * 4 * D_MODEL * D_MODEL),
        name="ada",
    )(c_all, w, b)


def _pre_kernel(x_ref, sc_ref, sh_ref, g_ref, sp_ref, mu_ref, muw_ref, mua_ref, w1_ref, a1_ref,
                wr_ref, wk_ref, wv_ref, wz_ref, w2_ref, a2_ref, w0_ref, a0_ref, kk_ref, ka_ref,
                segw_ref,
                r_out, kkn_out, dec_out, nb_out, km_out, v_out, z_out, shift_out,
                xs_scr, lw_scr, la_scr, carry_scr):
    t = pl.program_id(1)
    n = pl.program_id(2)
    last_t = pl.num_programs(1) - 1

    @pl.when(n == 0)
    def _():
        @pl.when(t == 0)
        def _():
            carry_scr[...] = sp_ref[...]

        h = _rms_mod(x_ref[...], g_ref[...], sc_ref[...], sh_ref[...])
        tm = h.shape[0]
        row = lax.broadcasted_iota(jnp.int32, h.shape, 0)
        h_prev = jnp.where(row == 0, carry_scr[...], pltpu.roll(h, 1, axis=0))
        dx = h_prev - h
        for p in range(4):
            xs_scr[p] = (h + mu_ref[p:p + 1, :] * dx).astype(BF16)
        xw = (h + dx * muw_ref[...]).astype(BF16)
        lw_scr[...] = jnp.tanh(jnp.dot(xw, w1_ref[...], preferred_element_type=F32))
        xa = (h + dx * mua_ref[...]).astype(BF16)
        la_scr[...] = jnp.dot(xa, a1_ref[...], preferred_element_type=F32)
        carry_scr[...] = h[tm - 1:tm, :]

        @pl.when(t == last_t)
        def _():
            shift_out[...] = h[tm - 1:tm, :]

    r = jnp.dot(xs_scr[0], wr_ref[...], preferred_element_type=F32)
    k = jnp.dot(xs_scr[1], wk_ref[...], preferred_element_type=F32)
    v = jnp.dot(xs_scr[2], wv_ref[...], preferred_element_type=F32)
    z = jnp.dot(xs_scr[3], wz_ref[...], preferred_element_type=F32)
    wl = w0_ref[...] + jnp.dot(lw_scr[...].astype(BF16), w2_ref[...], preferred_element_type=F32)
    dec = jnp.exp(-jnp.exp(_log_sigmoid(wl) - 0.5))
    a = _sigmoid(a0_ref[...] + jnp.dot(la_scr[...].astype(BF16), a2_ref[...],
                                       preferred_element_type=F32))
    kk = k * kk_ref[...]
    kkn = kk * lax.rsqrt(_seg_sum(kk * kk, segw_ref[...]) + L2_EPS)
    r_out[...] = r
    kkn_out[...] = kkn
    dec_out[...] = dec
    nb_out[...] = -(kkn * a)
    km_out[...] = k * (1.0 + (a - 1.0) * ka_ref[...])
    v_out[...] = v
    z_out[...] = z


def _rwkv_pre(x, scale, shift, shift_prev, p, seg64, tm, tn=512):
    b, t, _ = x.shape
    nt = MIX_WIDTH // tn
    row = lambda w: pl.BlockSpec((1, w), lambda i, j, n: (0, 0))
    per_b = pl.BlockSpec((None, 1, D_MODEL), lambda i, j, n: (i, 0, 0))
    col = lambda rows: pl.BlockSpec((rows, tn), lambda i, j, n: (0, n))
    w_in_part = lambda part: pl.BlockSpec((D_MODEL, tn), lambda i, j, n: (0, part * nt + n))
    out_blk = pl.BlockSpec((tm, tn), lambda i, j, n: (j, i * nt + n))
    act = jax.ShapeDtypeStruct((t, b * MIX_WIDTH), F32)
    est = (2 * tm * D_MODEL * 4 + 4 * tm * D_MODEL * 2 + 2 * 4 * D_MODEL * tn * 2
           + 2 * 7 * tm * tn * 4 + 8 * tm * tn * 4)
    return pl.pallas_call(
        _pre_kernel,
        grid=(b, t // tm, nt),
        in_specs=[pl.BlockSpec((None, tm, D_MODEL), lambda i, j, n: (i, j, 0)),
                  per_b, per_b, row(D_MODEL), per_b,
                  pl.BlockSpec((4, D_MODEL), lambda i, j, n: (0, 0)), row(D_MODEL), row(D_MODEL),
                  pl.BlockSpec((D_MODEL, LORA_RANK), lambda i, j, n: (0, 0)),
                  pl.BlockSpec((D_MODEL, LORA_RANK), lambda i, j, n: (0, 0)),
                  w_in_part(0), w_in_part(1), w_in_part(2), w_in_part(3),
                  col(LORA_RANK), col(LORA_RANK), col(1), col(1), col(1), col(1),
                  pl.BlockSpec(seg64.shape, lambda i, j, n: (0, 0))],
        out_specs=[out_blk] * 7 + [per_b],
        out_shape=[act] * 7 + [jax.ShapeDtypeStruct((b, 1, D_MODEL), F32)],
        scratch_shapes=[pltpu.VMEM((4, tm, D_MODEL), BF16),
                        pltpu.VMEM((tm, LORA_RANK), F32),
                        pltpu.VMEM((tm, LORA_RANK), F32),
                        pltpu.VMEM((1, D_MODEL), F32)],
        compiler_params=_params(("arbitrary", "arbitrary", "arbitrary"), est),
        name="rwkv_pre",
    )(x, scale, shift, p['norm_g'], shift_prev, p['mu_in'], p['mu_w'], p['mu_a'], p['w1'], p['a1'],
      p['w_in'], p['w_in'], p['w_in'], p['w_in'], p['w2'], p['a2'], p['w0'], p['a0'],
      p['k_k'], p['k_a'], seg64)


def _rec_kernel(kk_ref, dec_ref, nb_ref, km_ref, r_ref, v_ref, s0_ref, y_out, s_scr,
                kk_t, dec_t, nb_t, km_t, r_t, v_t, *, tb, chunks):
    step = pl.program_id(0)

    @pl.when(step == 0)
    def _():
        s_scr[...] = s0_ref[...]

    jobs = ((kk_ref, kk_t), (dec_ref, dec_t), (nb_ref, nb_t), (km_ref, km_t), (r_ref, r_t),
            (v_ref, v_t))
    row_groups = RWKV_HEAD // V7X_SUBLANES

    def to_lanes(job, i, c, slot):
        ref, dst = job
        xt = ref[i, c * PAIR_ROWS:(c + 1) * PAIR_ROWS, :].T
        dst[slot, c] = jnp.concatenate([xt[:RWKV_HEAD], xt[RWKV_HEAD:]], axis=1)

    def key_sum(p):
        acc = p[0:V7X_SUBLANES]
        for j in range(1, row_groups):
            acc = acc + p[j * V7X_SUBLANES:(j + 1) * V7X_SUBLANES]
        for shift in (4, 2, 1):
            acc = acc + pltpu.roll(acc, shift, axis=0)
        return acc

    def rows64(x8):
        return jnp.concatenate([x8] * row_groups, axis=0)

    sub = lax.broadcasted_iota(jnp.int32, (V7X_SUBLANES, CHUNK_LANES), 0)

    for c in range(chunks):
        for job in jobs:
            to_lanes(job, 0, c, 0)

    def token_step(i, carry):
        slot = i % 2
        i_next = jnp.minimum(i + 1, tb - 1)
        for c in range(chunks):
            y_rows = []
            for vo in range(row_groups):
                if vo < len(jobs):
                    to_lanes(jobs[vo], i_next, c, 1 - slot)
                v8 = v_t[slot, c, vo * V7X_SUBLANES:(vo + 1) * V7X_SUBLANES, :]
                y8 = jnp.zeros((V7X_SUBLANES, CHUNK_LANES), F32)
                for vi in range(V7X_SUBLANES):
                    row = vo * V7X_SUBLANES + vi
                    s_old = s_scr[c, row]
                    s_kk = rows64(key_sum(s_old * kk_t[slot, c]))
                    v_b = jnp.broadcast_to(v8[vi:vi + 1], (RWKV_HEAD, CHUNK_LANES))
                    s_new = s_old * dec_t[slot, c] + s_kk * nb_t[slot, c] + v_b * km_t[slot, c]
                    s_scr[c, row] = s_new
                    y8 = jnp.where(sub == vi, key_sum(s_new * r_t[slot, c]), y8)
                y_rows.append(y8)
            y = jnp.concatenate(y_rows, axis=0)
            y_out[i, c * PAIR_ROWS:(c + 1) * PAIR_ROWS, :] = jnp.concatenate(
                [y[:, :PAIR_ROWS], y[:, PAIR_ROWS:]], axis=0).T
        return carry

    lax.fori_loop(0, tb, token_step, 0)


def _rwkv_recurrence(kkn, dec, nb, km, r, v, s0, tb):
    t, pairs, _ = kkn.shape
    chunks = pairs // PAIR_ROWS
    act_blk = pl.BlockSpec((tb, pairs, 2 * RWKV_HEAD), lambda j: (j, 0, 0))
    state_blk = pl.BlockSpec(s0.shape, lambda j: (0, 0, 0, 0))
    tile = pltpu.VMEM((2, chunks, RWKV_HEAD, CHUNK_LANES), F32)
    state_bytes = s0.size * 4
    est = 2 * 7 * tb * pairs * 2 * RWKV_HEAD * 4 + 4 * state_bytes
    return pl.pallas_call(
        functools.partial(_rec_kernel, tb=tb, chunks=chunks),
        grid=(t // tb,),
        in_specs=[act_blk] * 6 + [state_blk],
        out_specs=[act_blk, state_blk],
        out_shape=[jax.ShapeDtypeStruct(kkn.shape, F32), jax.ShapeDtypeStruct(s0.shape, F32)],
        scratch_shapes=[tile] * 6,
        compiler_params=_params(("arbitrary",), est),
        name="rwkv_recurrence",
    )(kkn, dec, nb, km, r, v, s0)


def _post_kernel(y_ref, r_ref, km_ref, v_ref, z_ref, x_ref, gate_ref, lng_ref, lnb_ref, rk_ref,
                 wout_ref, segw_ref, o_ref):
    seg_w = segw_ref[...]
    inv = 1.0 / RWKV_HEAD
    y = y_ref[...]
    d = y - _seg_sum(y, seg_w) * inv
    var = _seg_sum(d * d, seg_w) * inv
    yn = d * lax.rsqrt(var + GN_EPS) * lng_ref[...] + lnb_ref[...]
    bonus = _seg_sum(r_ref[...] * km_ref[...] * rk_ref[...], seg_w) * v_ref[...]
    z = z_ref[...]
    gated = ((yn + bonus) * (z * _sigmoid(z))).astype(BF16)
    o_ref[...] = x_ref[...] + gate_ref[...] * jnp.dot(gated, wout_ref[...],
                                                      preferred_element_type=F32)


def _rwkv_post(y, r, km, v, z, x, gate, p, seg64, tm):
    b, t, _ = x.shape
    act_blk = pl.BlockSpec((tm, MIX_WIDTH), lambda i, j: (j, i))
    x_blk = pl.BlockSpec((None, tm, D_MODEL), lambda i, j: (i, j, 0))
    row = pl.BlockSpec((1, MIX_WIDTH), lambda i, j: (0, 0))
    est = 2 * 5 * tm * MIX_WIDTH * 4 + 2 * MIX_WIDTH * D_MODEL * 2 + 12 * tm * MIX_WIDTH * 4
    return pl.pallas_call(
        _post_kernel,
        grid=(b, t // tm),
        in_specs=[act_blk] * 5 + [x_blk, pl.BlockSpec((None, 1, D_MODEL), lambda i, j: (i, 0, 0)),
                                  row, row, row,
                                  pl.BlockSpec((MIX_WIDTH, D_MODEL), lambda i, j: (0, 0)),
                                  pl.BlockSpec(seg64.shape, lambda i, j: (0, 0))],
        out_specs=x_blk,
        out_shape=jax.ShapeDtypeStruct(x.shape, F32),
        compiler_params=_params(("arbitrary", "arbitrary"), est),
        name="rwkv_post",
    )(y, r, km, v, z, x, gate, p['ln_g'], p['ln_b'], p['r_k'], p['w_out'], seg64)


def _norm_proj_kernel(x_ref, sc_ref, sh_ref, g_ref, wa_ref, wb_ref, gain_ref, segw_ref,
                      a_out, b_out, lhs_scr):
    @pl.when(pl.program_id(2) == 0)
    def _():
        lhs_scr[...] = _rms_mod(x_ref[...], g_ref[...], sc_ref[...], sh_ref[...]).astype(BF16)

    lhs = lhs_scr[...]
    a = jnp.dot(lhs, wa_ref[...], preferred_element_type=F32)
    ms = _seg_sum(a * a, segw_ref[...]) * (1.0 / SB_HEAD)
    a_out[...] = a * lax.rsqrt(ms + RMS_EPS) * gain_ref[...]
    b_out[...] = jnp.dot(lhs, wb_ref[...], preferred_element_type=F32)


def _norm_proj(x, scale, shift, g, w, gain_row, seg128, tm, tn=512):
    b, t, _ = x.shape
    nt = MIX_WIDTH // tn
    per_b = pl.BlockSpec((None, 1, D_MODEL), lambda i, j, n: (i, 0, 0))
    out_blk = pl.BlockSpec((None, tm, tn), lambda i, j, n: (i, j, n))
    act = jax.ShapeDtypeStruct((b, t, MIX_WIDTH), F32)
    est = 2 * tm * D_MODEL * 4 + tm * D_MODEL * 2 + 4 * D_MODEL * tn * 2 + 10 * tm * tn * 4
    return pl.pallas_call(
        _norm_proj_kernel,
        grid=(b, t // tm, nt),
        in_specs=[pl.BlockSpec((None, tm, D_MODEL), lambda i, j, n: (i, j, 0)), per_b, per_b,
                  pl.BlockSpec((1, D_MODEL), lambda i, j, n: (0, 0)),
                  pl.BlockSpec((D_MODEL, tn), lambda i, j, n: (0, n)),
                  pl.BlockSpec((D_MODEL, tn), lambda i, j, n: (0, nt + n)),
                  pl.BlockSpec((1, tn), lambda i, j, n: (0, 0)),
                  pl.BlockSpec(seg128.shape, lambda i, j, n: (0, 0))],
        out_specs=[out_blk, out_blk],
        out_shape=[act, act],
        scratch_shapes=[pltpu.VMEM((tm, D_MODEL), BF16)],
        compiler_params=_params(("arbitrary", "arbitrary", "arbitrary"), est),
        name="norm_proj",
    )(x, scale, shift, g, w, w, gain_row, seg128)


def _attn_kernel(q_ref, k_ref, v_ref, wat_ref, o_ref, kb_scr, vb_scr, z_scr, w_scr, acc_scr,
                 *, tq, offset, top_steps):
    i = pl.program_id(2)

    @pl.when(i == 0)
    def _():
        kb_scr[...] = k_ref[...].astype(BF16)
        vb_scr[...] = v_ref[...].astype(BF16)

    q = (q_ref[...] * (SB_SCALE * LOG2_E)).astype(BF16)
    q_first = offset + i * tq
    first = q_first // KEY_STEP
    wat = wat_ref[...]

    def at_step(ref, ks):
        return ref[pl.ds(pl.multiple_of(ks * KEY_STEP, KEY_STEP), KEY_STEP), :]

    def logits(q_rows, ks):
        return lax.dot_general(q_rows, at_step(kb_scr, ks), (((1,), (1,)), ((), ())),
                               preferred_element_type=F32)

    def stick(z2, carry, valid):
        cost = jnp.maximum(z2, 0.0) + jnp.log2(1.0 + jnp.exp2(-jnp.abs(z2)))
        if valid is not None:
            cost = jnp.where(valid, cost, 0.0)
        tail_in = jnp.dot(_split_hi_lo(cost), wat, preferred_element_type=F32)
        weights = jnp.exp2(z2 - cost - (tail_in + carry))
        if valid is not None:
            weights = jnp.where(valid, weights, 0.0)
        return weights.astype(BF16), carry + (tail_in[:, 0:1] + cost[:, 0:1])

    acc_scr[...] = jnp.zeros_like(acc_scr)
    carry = None
    for j in reversed(range(top_steps)):
        lo = j * KEY_STEP
        rows = tq - lo
        z2 = logits(q[lo:], first + j)
        key_pos = (first + j) * KEY_STEP + lax.broadcasted_iota(jnp.int32, (rows, KEY_STEP), 1)
        q_pos = q_first + lo + lax.broadcasted_iota(jnp.int32, (rows, KEY_STEP), 0)
        seen = 0 if carry is None else carry.shape[0]
        prev = jnp.zeros((rows - seen, 1), F32)
        if carry is not None:
            prev = jnp.concatenate([prev, carry], axis=0)
        weights, carry = stick(z2, prev, key_pos < q_pos)
        if j > 0:
            acc_scr[lo:, :] += jnp.dot(weights, at_step(vb_scr, first + j),
                                       preferred_element_type=F32)
        else:
            w_scr[...] = weights

    z_scr[...] = logits(q, jnp.maximum(first - 1, 0))

    def staged(n, state):
        carry, ks_prev = state
        ks = first - 1 - n
        acc_scr[...] += jnp.dot(w_scr[...], at_step(vb_scr, ks_prev), preferred_element_type=F32)
        weights, carry = stick(z_scr[...], carry, None)
        w_scr[...] = weights
        z_scr[...] = logits(q, jnp.maximum(ks - 1, 0))
        return carry, ks

    _, ks_prev = lax.fori_loop(0, first, staged, (carry, first))
    o_ref[...] = acc_scr[...] + jnp.dot(w_scr[...], at_step(vb_scr, ks_prev),
                                        preferred_element_type=F32)


def _attn_weights():
    j = jnp.arange(2 * KEY_STEP) % KEY_STEP
    s = jnp.arange(KEY_STEP)
    return (j[:, None] > s[None, :]).astype(BF16)


def _sb_attention(q, k_all, v_all, offset, wat, tq):
    b, t, _ = q.shape
    tk = k_all.shape[1]
    top_steps = -(-tq // KEY_STEP)
    assert offset % KEY_STEP == 0 and tk % KEY_STEP == 0 and tk >= offset + t
    assert tq % KEY_STEP == 0 or t == tq
    q_blk = pl.BlockSpec((None, tq, SB_HEAD), lambda i, h, j: (i, j, h))
    kv_blk = pl.BlockSpec((None, tk, SB_HEAD), lambda i, h, j: (i, 0, h))
    est = 2 * 2 * tk * SB_HEAD * 4 + 2 * tk * SB_HEAD * 2 + 24 * tq * KEY_STEP * 4
    return pl.pallas_call(
        functools.partial(_attn_kernel, tq=tq, offset=offset, top_steps=top_steps),
        grid=(b, SB_HEADS, t // tq),
        in_specs=[q_blk, kv_blk, kv_blk, pl.BlockSpec(wat.shape, lambda i, h, j: (0, 0))],
        out_specs=q_blk,
        out_shape=jax.ShapeDtypeStruct(q.shape, F32),
        scratch_shapes=[pltpu.VMEM((tk, SB_HEAD), BF16), pltpu.VMEM((tk, SB_HEAD), BF16),
                        pltpu.VMEM((tq, KEY_STEP), F32), pltpu.VMEM((tq, KEY_STEP), BF16),
                        pltpu.VMEM((tq, SB_HEAD), F32)],
        compiler_params=_params(("arbitrary", "arbitrary", "arbitrary"), est),
        name="sb_attention",
    )(q, k_all, v_all, wat)


def _gated_out_kernel(o_ref, z_ref, x_ref, gate_ref, w_ref, out_ref):
    z = z_ref[...]
    lhs = (o_ref[...] * (z * _sigmoid(z))).astype(BF16)
    out_ref[...] = x_ref[...] + gate_ref[...] * jnp.dot(lhs, w_ref[...],
                                                        preferred_element_type=F32)


def _gated_out(o, z, x, gate, w, tm):
    b, t, _ = x.shape
    act_blk = pl.BlockSpec((None, tm, MIX_WIDTH), lambda i, j: (i, j, 0))
    x_blk = pl.BlockSpec((None, tm, D_MODEL), lambda i, j: (i, j, 0))
    est = 2 * 2 * tm * MIX_WIDTH * 4 + 2 * MIX_WIDTH * D_MODEL * 2 + 4 * tm * MIX_WIDTH * 4
    return pl.pallas_call(
        _gated_out_kernel,
        grid=(b, t // tm),
        in_specs=[act_blk, act_blk, x_blk,
                  pl.BlockSpec((None, 1, D_MODEL), lambda i, j: (i, 0, 0)),
                  pl.BlockSpec((MIX_WIDTH, D_MODEL), lambda i, j: (0, 0))],
        out_specs=x_blk,
        out_shape=jax.ShapeDtypeStruct(x.shape, F32),
        compiler_params=_params(("arbitrary", "arbitrary"), est),
        name="gated_out",
    )(o, z, x, gate, w)


def _state_to_lanes(s):
    problems = s.shape[0] * s.shape[1]
    s = s.reshape(problems // CHUNK_LANES, PAIR_ROWS, 2, RWKV_HEAD, RWKV_HEAD)
    return s.transpose(0, 3, 4, 2, 1).reshape(-1, RWKV_HEAD, RWKV_HEAD, CHUNK_LANES)


def _lanes_to_state(s, b):
    s = s.reshape(-1, RWKV_HEAD, RWKV_HEAD, 2, PAIR_ROWS)
    return s.transpose(0, 4, 3, 1, 2).reshape(b, RWKV_HEADS, RWKV_HEAD, RWKV_HEAD)


def _trunk(x, ada_a, ada_b, past_k, past_v, shift0, wkv0, pa, pb, consts):
    b, t, _ = x.shape
    seg64, seg128, wat = consts
    split3 = lambda a: [v[:, None, :] for v in jnp.split(a, 3, axis=-1)]
    shift_a, scale_a, gate_a = split3(ada_a)
    shift_b, scale_b, gate_b = split3(ada_b)

    r, kkn, dec, nb, km, v, z, shift_new = _rwkv_pre(
        x, scale_a, shift_a, shift0[:, None, :], pa, seg64, tm=min(t, 512))
    paired = lambda a: a.reshape(t, b * RWKV_HEADS // 2, 2 * RWKV_HEAD)
    y, s_new = _rwkv_recurrence(paired(kkn), paired(dec), paired(nb), paired(km), paired(r),
                                paired(v), _state_to_lanes(wkv0), tb=min(t, 8))
    x_mid = _rwkv_post(y.reshape(t, b * MIX_WIDTH), r, km, v, z, x, gate_a, pa, seg64,
                       tm=min(t, 128))

    zeros = jnp.zeros((b, 1, D_MODEL), F32)
    k_new, v_new = _norm_proj(x_mid, zeros, zeros, pb['kv_norm_g'], pb['kv_w'], pb['k_gain'],
                              seg128, tm=min(t, 512))
    q, z1 = _norm_proj(x_mid, scale_b, shift_b, pb['norm_g'], pb['w_in'], pb['q_gain'],
                       seg128, tm=min(t, 512))
    if past_k is None:
        offset, k_all, v_all = 0, k_new, v_new
    else:
        offset = past_k.shape[1]
        pad = (-(offset + t)) % KEY_STEP
        flat = lambda c: c.reshape(b, offset, MIX_WIDTH)
        tail = jnp.zeros((b, pad, MIX_WIDTH), F32)
        k_all = jnp.concatenate([flat(past_k), k_new, tail], axis=1)
        v_all = jnp.concatenate([flat(past_v), v_new, tail], axis=1)
    o = _sb_attention(q, k_all, v_all, offset, wat, tq=min(t, 512))
    x_out = _gated_out(o, z1, x_mid, gate_b, pb['w_out'], tm=min(t, 256))

    heads = lambda a: a.reshape(b, t, SB_HEADS, SB_HEAD)
    return (x_out, heads(k_new), heads(v_new), _lanes_to_state(s_new, b)[None],
            shift_new.reshape(1, b, D_MODEL))


def kernel(x_prompt, x_sample, cache_k, cache_v, state_wkv, state_shift, c_prompt, c_sample, a_norm_g, a_ada_w, a_ada_b, a_w_in, a_mu_in, a_mu_w, a_mu_a, a_w0, a_w1, a_w2, a_a0, a_a1, a_a2, a_k_k, a_k_a, a_r_k, a_ln_g, a_ln_b, a_w_out, kv_norm_g, kv_w, k_gain, b_norm_g, b_ada_w, b_ada_b, b_w_in, b_q_gain, b_w_out):
    bp = x_prompt.shape[0]
    row = lambda a: a.reshape(1, -1)
    pa = dict(norm_g=a_norm_g, mu_in=a_mu_in[0], mu_w=a_mu_w, mu_a=a_mu_a,
              w_in=a_w_in[0].astype(BF16), w1=a_w1[0].astype(BF16), a1=a_a1[0].astype(BF16),
              w2=a_w2[0].astype(BF16), a2=a_a2[0].astype(BF16), w0=a_w0, a0=a_a0,
              k_k=a_k_k, k_a=a_k_a, r_k=row(a_r_k[0]), ln_g=a_ln_g, ln_b=a_ln_b,
              w_out=a_w_out[0].astype(BF16))
    heads_per_tile = 512 // SB_HEAD
    pb = dict(kv_norm_g=row(kv_norm_g), kv_w=kv_w.astype(BF16),
              k_gain=jnp.tile(row(k_gain), (1, heads_per_tile)),
              norm_g=b_norm_g, w_in=b_w_in[0].astype(BF16),
              q_gain=jnp.tile(b_q_gain, (1, heads_per_tile)),
              w_out=b_w_out[0].astype(BF16))
    consts = (_seg_weights(RWKV_HEAD), _seg_weights(SB_HEAD), _attn_weights())

    ada = _ada(jnp.concatenate([c_prompt, c_sample], axis=0),
               jnp.concatenate([a_ada_w, b_ada_w], axis=0),
               jnp.concatenate([a_ada_b, b_ada_b], axis=0)[:, None, :])
    zero_shift = jnp.zeros((bp, D_MODEL), F32)
    zero_wkv = jnp.zeros((bp,) + state_wkv.shape[2:], F32)
    y_p, k_p, v_p, wkv_p, shift_p = _trunk(
        x_prompt, ada[0, :bp], ada[1, :bp], None, None, zero_shift, zero_wkv, pa, pb, consts)
    y_s, k_s, v_s, wkv_s, shift_s = _trunk(
        x_sample, ada[0, bp:], ada[1, bp:], cache_k, cache_v, state_shift[0], state_wkv[0],
        pa, pb, consts)
    return (y_p, y_s, k_p, v_p, wkv_p, shift_p, k_s, v_s, wkv_s, shift_s)
```

```python
import functools

import jax
import jax.numpy as jnp
from jax import lax
from jax.experimental import pallas as pl
from jax.experimental.pallas import tpu as pltpu

F32 = jnp.float32
BF16 = jnp.bfloat16

D_MODEL = 1024
MIX_WIDTH = 2048
RWKV_HEAD = 64
RWKV_HEADS = MIX_WIDTH // RWKV_HEAD
SB_HEAD = 128
SB_HEADS = MIX_WIDTH // SB_HEAD
LORA_RANK = 64
RMS_EPS = 1e-6
GN_EPS = 64e-5
L2_EPS = 1e-12
SB_SCALE = SB_HEAD ** -0.5
LOG2_E = 1.4426950408889634

V7X_LANES = 128
V7X_SUBLANES = 8
V7X_MXU_WIDTH = 256
V7X_VMEM_BYTES = 64 * 1024 * 1024

SEG_TILE = V7X_MXU_WIDTH
PAIR_ROWS = V7X_LANES
CHUNK_LANES = 2 * PAIR_ROWS
KEY_STEP = V7X_MXU_WIDTH


def _vmem_limit(estimate_bytes):
    return int(min(V7X_VMEM_BYTES - 8 * 1024 * 1024, max(32 * 1024 * 1024, 2 * estimate_bytes)))


def _params(semantics, estimate_bytes):
    return pltpu.CompilerParams(dimension_semantics=semantics,
                                vmem_limit_bytes=_vmem_limit(estimate_bytes))


def _split_hi_lo(x):
    hi = x.astype(BF16)
    lo = (x - hi.astype(F32)).astype(BF16)
    return jnp.concatenate([hi, lo], axis=-1)


def _seg_sum(x, seg_w):
    m, c = x.shape
    tiles = c // SEG_TILE
    stacked = jnp.concatenate(
        [_split_hi_lo(x[:, j * SEG_TILE:(j + 1) * SEG_TILE]) for j in range(tiles)], axis=0)
    res = jnp.dot(stacked, seg_w, preferred_element_type=F32)
    return jnp.concatenate([res[j * m:(j + 1) * m] for j in range(tiles)], axis=1)


def _seg_weights(seg):
    j = jnp.arange(2 * SEG_TILE) % SEG_TILE
    l = jnp.arange(SEG_TILE)
    return ((j[:, None] // seg) == (l[None, :] // seg)).astype(BF16)


def _to_tiles(val, ref):
    m, c = val.shape
    grouped = val.reshape(m // V7X_SUBLANES, V7X_SUBLANES, c)
    for p in range(c // V7X_LANES):
        ref[:, p * V7X_SUBLANES:(p + 1) * V7X_SUBLANES, :] = grouped[:, :, p * V7X_LANES:(p + 1) * V7X_LANES]


def _from_tiles(ref):
    tiles = ref[...]
    groups, rows, _ = tiles.shape
    return jnp.concatenate(
        [tiles[:, p * V7X_SUBLANES:(p + 1) * V7X_SUBLANES, :].reshape(groups * V7X_SUBLANES, V7X_LANES)
         for p in range(rows // V7X_SUBLANES)], axis=1)


def _log_sigmoid(x):
    return jnp.minimum(x, 0.0) - jnp.log(1.0 + jnp.exp(-jnp.abs(x)))


def _sigmoid(x):
    return 1.0 / (1.0 + jnp.exp(-x))


def _rms_mod(x, g, scale, shift):
    ms = jnp.mean(x * x, axis=-1, keepdims=True)
    return (x * lax.rsqrt(ms + RMS_EPS)) * g * (1.0 + scale) + shift


def _ada_kernel(c_ref, w_ref, b_ref, o_ref):
    o_ref[...] = jnp.dot(c_ref[...], w_ref[...], precision=lax.Precision.HIGHEST,
                         preferred_element_type=F32) + b_ref[...]


def _ada(c_all, w, b):
    m = c_all.shape[0]
    layers = w.shape[0]
    return pl.pallas_call(
        _ada_kernel,
        grid=(layers, 3),
        in_specs=[pl.BlockSpec((m, D_MODEL), lambda l, n: (0, 0)),
                  pl.BlockSpec((None, D_MODEL, D_MODEL), lambda l, n: (l, 0, n)),
                  pl.BlockSpec((None, 1, D_MODEL), lambda l, n: (l, 0, n))],
        out_specs=pl.BlockSpec((None, m, D_MODEL), lambda l, n: (l, 0, n)),
        out_shape=jax.ShapeDtypeStruct((layers, m, 3 * D_MODEL), F32),
        compiler_params=_params(("arbitrary", "arbitrary"), 2 * 4 * D_MODEL * D_MODEL),
        name="ada",
    )(c_all, w, b)


def _pre_kernel(x_ref, sc_ref, sh_ref, g_ref, sp_ref, mu_ref, muw_ref, mua_ref, w1_ref, a1_ref,
                wr_ref, wk_ref, wv_ref, wz_ref, w2_ref, a2_ref, w0_ref, a0_ref, kk_ref, ka_ref,
                segw_ref,
                r_out, kkn_out, dec_out, nb_out, km_out, v_out, z_out, shift_out,
                xs_scr, lw_scr, la_scr, carry_scr):
    t = pl.program_id(1)
    n = pl.program_id(2)
    last_t = pl.num_programs(1) - 1

    @pl.when(n == 0)
    def _():
        @pl.when(t == 0)
        def _():
            carry_scr[...] = sp_ref[...]

        h = _rms_mod(x_ref[...], g_ref[...], sc_ref[...], sh_ref[...])
        tm = h.shape[0]
        row = lax.broadcasted_iota(jnp.int32, h.shape, 0)
        h_prev = jnp.where(row == 0, carry_scr[...], pltpu.roll(h, 1, axis=0))
        dx = h_prev - h
        for p in range(4):
            xs_scr[p] = (h + mu_ref[p:p + 1, :] * dx).astype(BF16)
        xw = (h + dx * muw_ref[...]).astype(BF16)
        lw_scr[...] = jnp.tanh(jnp.dot(xw, w1_ref[...], preferred_element_type=F32))
        xa = (h + dx * mua_ref[...]).astype(BF16)
        la_scr[...] = jnp.dot(xa, a1_ref[...], preferred_element_type=F32)
        carry_scr[...] = h[tm - 1:tm, :]

        @pl.when(t == last_t)
        def _():
            shift_out[...] = h[tm - 1:tm, :]

    r = jnp.dot(xs_scr[0], wr_ref[...], preferred_element_type=F32)
    k = jnp.dot(xs_scr[1], wk_ref[...], preferred_element_type=F32)
    v = jnp.dot(xs_scr[2], wv_ref[...], preferred_element_type=F32)
    z = jnp.dot(xs_scr[3], wz_ref[...], preferred_element_type=F32)
    wl = w0_ref[...] + jnp.dot(lw_scr[...].astype(BF16), w2_ref[...], preferred_element_type=F32)
    dec = jnp.exp(-jnp.exp(_log_sigmoid(wl) - 0.5))
    a = _sigmoid(a0_ref[...] + jnp.dot(la_scr[...].astype(BF16), a2_ref[...],
                                       preferred_element_type=F32))
    kk = k * kk_ref[...]
    kkn = kk * lax.rsqrt(_seg_sum(kk * kk, segw_ref[...]) + L2_EPS)
    _to_tiles(r, r_out)
    _to_tiles(kkn, kkn_out)
    _to_tiles(dec, dec_out)
    _to_tiles(-(kkn * a), nb_out)
    _to_tiles(k * (1.0 + (a - 1.0) * ka_ref[...]), km_out)
    _to_tiles(v, v_out)
    _to_tiles(z, z_out)


def _tiled_shape(t, b):
    return (t // V7X_SUBLANES, b * (MIX_WIDTH // V7X_LANES) * V7X_SUBLANES, V7X_LANES)


def _rwkv_pre(x, scale, shift, shift_prev, p, seg64, tm, tn=512):
    b, t, _ = x.shape
    nt = MIX_WIDTH // tn
    row = lambda w: pl.BlockSpec((1, w), lambda i, j, n: (0, 0))
    per_b = pl.BlockSpec((None, 1, D_MODEL), lambda i, j, n: (i, 0, 0))
    col = lambda rows: pl.BlockSpec((rows, tn), lambda i, j, n: (0, n))
    w_in_part = lambda part: pl.BlockSpec((D_MODEL, tn), lambda i, j, n: (0, part * nt + n))
    out_blk = pl.BlockSpec((tm // V7X_SUBLANES, tn // V7X_LANES * V7X_SUBLANES, V7X_LANES),
                           lambda i, j, n: (j, i * nt + n, 0))
    act = jax.ShapeDtypeStruct(_tiled_shape(t, b), F32)
    est = (2 * tm * D_MODEL * 4 + 4 * tm * D_MODEL * 2 + 2 * 4 * D_MODEL * tn * 2
           + 2 * 7 * tm * tn * 4 + 8 * tm * tn * 4)
    return pl.pallas_call(
        _pre_kernel,
        grid=(b, t // tm, nt),
        in_specs=[pl.BlockSpec((None, tm, D_MODEL), lambda i, j, n: (i, j, 0)),
                  per_b, per_b, row(D_MODEL), per_b,
                  pl.BlockSpec((4, D_MODEL), lambda i, j, n: (0, 0)), row(D_MODEL), row(D_MODEL),
                  pl.BlockSpec((D_MODEL, LORA_RANK), lambda i, j, n: (0, 0)),
                  pl.BlockSpec((D_MODEL, LORA_RANK), lambda i, j, n: (0, 0)),
                  w_in_part(0), w_in_part(1), w_in_part(2), w_in_part(3),
                  col(LORA_RANK), col(LORA_RANK), col(1), col(1), col(1), col(1),
                  pl.BlockSpec(seg64.shape, lambda i, j, n: (0, 0))],
        out_specs=[out_blk] * 7 + [per_b],
        out_shape=[act] * 7 + [jax.ShapeDtypeStruct((b, 1, D_MODEL), F32)],
        scratch_shapes=[pltpu.VMEM((4, tm, D_MODEL), BF16),
                        pltpu.VMEM((tm, LORA_RANK), F32),
                        pltpu.VMEM((tm, LORA_RANK), F32),
                        pltpu.VMEM((1, D_MODEL), F32)],
        compiler_params=_params(("arbitrary", "arbitrary", "arbitrary"), est),
        name="rwkv_pre",
    )(x, scale, shift, p['norm_g'], shift_prev, p['mu_in'], p['mu_w'], p['mu_a'], p['w1'], p['a1'],
      p['w_in'], p['w_in'], p['w_in'], p['w_in'], p['w2'], p['a2'], p['w0'], p['a0'],
      p['k_k'], p['k_a'], seg64)


def _rec_kernel(kk_ref, dec_ref, nb_ref, km_ref, r_ref, v_ref, s0_ref, y_out, s_scr,
                kk_t, dec_t, nb_t, km_t, r_t, v_t, d_scr, kk_s, nb_s, km_s, r_s, *, tb, chunks):
    step = pl.program_id(0)

    @pl.when(step == 0)
    def _():
        s_scr[...] = s0_ref[...]

    jobs = ((kk_ref, kk_t), (dec_ref, dec_t), (nb_ref, nb_t), (km_ref, km_t), (r_ref, r_t),
            (v_ref, v_t))
    row_groups = RWKV_HEAD // V7X_SUBLANES

    def pair_rows(i, c):
        start = c * PAIR_ROWS * V7X_SUBLANES + i % V7X_SUBLANES
        return i // V7X_SUBLANES, pl.ds(start, PAIR_ROWS, stride=V7X_SUBLANES)

    def to_lanes(job, i, c, slot):
        ref, dst = job
        group, rows = pair_rows(i, c)
        xt = ref[group, rows, :].T
        dst[slot, c] = jnp.concatenate([xt[:RWKV_HEAD], xt[RWKV_HEAD:]], axis=1)

    def key_sum(p):
        acc = p[0:V7X_SUBLANES]
        for j in range(1, row_groups):
            acc = acc + p[j * V7X_SUBLANES:(j + 1) * V7X_SUBLANES]
        for shift in (4, 2, 1):
            acc = acc + pltpu.roll(acc, shift, axis=0)
        return acc

    def rows64(x8):
        return jnp.concatenate([x8] * row_groups, axis=0)

    sub = lax.broadcasted_iota(jnp.int32, (V7X_SUBLANES, CHUNK_LANES), 0)

    for c in range(chunks):
        for job in jobs:
            to_lanes(job, 0, c, 0)
        d_scr[c] = jnp.ones((RWKV_HEAD, CHUNK_LANES), F32)

    def token_step(i, carry):
        slot = i % 2
        i_next = jnp.minimum(i + 1, tb - 1)
        for c in range(chunks):
            d_prev = d_scr[c]
            d_now = d_prev * dec_t[slot, c]
            d_inv = 1.0 / d_now
            d_scr[c] = d_now
            kk_s[c] = kk_t[slot, c] * d_prev
            nb_s[c] = nb_t[slot, c] * d_inv
            km_s[c] = km_t[slot, c] * d_inv
            r_s[c] = r_t[slot, c] * d_now
            y_rows = []
            for vo in range(row_groups):
                if vo < len(jobs):
                    to_lanes(jobs[vo], i_next, c, 1 - slot)
                v8 = v_t[slot, c, vo * V7X_SUBLANES:(vo + 1) * V7X_SUBLANES, :]
                y8 = jnp.zeros((V7X_SUBLANES, CHUNK_LANES), F32)
                for vi in range(V7X_SUBLANES):
                    row = vo * V7X_SUBLANES + vi
                    s_old = s_scr[c, row]
                    s_kk = rows64(key_sum(s_old * kk_s[c]))
                    v_b = jnp.broadcast_to(v8[vi:vi + 1], (RWKV_HEAD, CHUNK_LANES))
                    s_new = s_old + s_kk * nb_s[c] + v_b * km_s[c]
                    s_scr[c, row] = s_new
                    y8 = jnp.where(sub == vi, key_sum(s_new * r_s[c]), y8)
                y_rows.append(y8)
            y = jnp.concatenate(y_rows, axis=0)
            group, rows = pair_rows(i, c)
            y_out[group, rows, :] = jnp.concatenate(
                [y[:, :PAIR_ROWS], y[:, PAIR_ROWS:]], axis=0).T
        return carry

    lax.fori_loop(0, tb, token_step, 0)

    def fold_decay(row, carry):
        for c in range(chunks):
            s_scr[c, row] = s_scr[c, row] * d_scr[c]
        return carry

    lax.fori_loop(0, RWKV_HEAD, fold_decay, 0)


def _rwkv_recurrence(kkn, dec, nb, km, r, v, s0, tb):
    groups, rows, _ = kkn.shape
    t = groups * V7X_SUBLANES
    chunks = rows // (PAIR_ROWS * V7X_SUBLANES)
    act_blk = pl.BlockSpec((tb // V7X_SUBLANES, rows, V7X_LANES), lambda j: (j, 0, 0))
    state_blk = pl.BlockSpec(s0.shape, lambda j: (0, 0, 0, 0))
    tile = pltpu.VMEM((2, chunks, RWKV_HEAD, CHUNK_LANES), F32)
    state_bytes = s0.size * 4
    est = 2 * 7 * tb * rows * V7X_LANES * 4 // V7X_SUBLANES + 4 * state_bytes
    return pl.pallas_call(
        functools.partial(_rec_kernel, tb=tb, chunks=chunks),
        grid=(t // tb,),
        in_specs=[act_blk] * 6 + [state_blk],
        out_specs=[act_blk, state_blk],
        out_shape=[jax.ShapeDtypeStruct(kkn.shape, F32), jax.ShapeDtypeStruct(s0.shape, F32)],
        scratch_shapes=[tile] * 6 + [pltpu.VMEM((chunks, RWKV_HEAD, CHUNK_LANES), F32)] * 5,
        compiler_params=_params(("arbitrary",), est),
        name="rwkv_recurrence",
    )(kkn, dec, nb, km, r, v, s0)


def _post_kernel(y_ref, r_ref, km_ref, v_ref, z_ref, x_ref, gate_ref, lng_ref, lnb_ref, rk_ref,
                 wout_ref, segw_ref, o_ref):
    seg_w = segw_ref[...]
    inv = 1.0 / RWKV_HEAD
    y = _from_tiles(y_ref)
    d = y - _seg_sum(y, seg_w) * inv
    var = _seg_sum(d * d, seg_w) * inv
    yn = d * lax.rsqrt(var + GN_EPS) * lng_ref[...] + lnb_ref[...]
    bonus = _seg_sum(_from_tiles(r_ref) * _from_tiles(km_ref) * rk_ref[...], seg_w) * _from_tiles(v_ref)
    z = _from_tiles(z_ref)
    gated = ((yn + bonus) * (z * _sigmoid(z))).astype(BF16)
    o_ref[...] = x_ref[...] + gate_ref[...] * jnp.dot(gated, wout_ref[...],
                                                      preferred_element_type=F32)


def _rwkv_post(y, r, km, v, z, x, gate, p, seg64, tm):
    b, t, _ = x.shape
    act_blk = pl.BlockSpec((tm // V7X_SUBLANES, MIX_WIDTH // V7X_LANES * V7X_SUBLANES, V7X_LANES),
                           lambda i, j: (j, i, 0))
    x_blk = pl.BlockSpec((None, tm, D_MODEL), lambda i, j: (i, j, 0))
    row = pl.BlockSpec((1, MIX_WIDTH), lambda i, j: (0, 0))
    est = 2 * 5 * tm * MIX_WIDTH * 4 + 2 * MIX_WIDTH * D_MODEL * 2 + 12 * tm * MIX_WIDTH * 4
    return pl.pallas_call(
        _post_kernel,
        grid=(b, t // tm),
        in_specs=[act_blk] * 5 + [x_blk, pl.BlockSpec((None, 1, D_MODEL), lambda i, j: (i, 0, 0)),
                                  row, row, row,
                                  pl.BlockSpec((MIX_WIDTH, D_MODEL), lambda i, j: (0, 0)),
                                  pl.BlockSpec(seg64.shape, lambda i, j: (0, 0))],
        out_specs=x_blk,
        out_shape=jax.ShapeDtypeStruct(x.shape, F32),
        compiler_params=_params(("arbitrary", "arbitrary"), est),
        name="rwkv_post",
    )(y, r, km, v, z, x, gate, p['ln_g'], p['ln_b'], p['r_k'], p['w_out'], seg64)


def _norm_proj_kernel(x_ref, sc_ref, sh_ref, g_ref, wa_ref, wb_ref, gain_ref, segw_ref,
                      a_out, b_out, lhs_scr):
    @pl.when(pl.program_id(2) == 0)
    def _():
        lhs_scr[...] = _rms_mod(x_ref[...], g_ref[...], sc_ref[...], sh_ref[...]).astype(BF16)

    lhs = lhs_scr[...]
    a = jnp.dot(lhs, wa_ref[...], preferred_element_type=F32)
    ms = _seg_sum(a * a, segw_ref[...]) * (1.0 / SB_HEAD)
    a_out[...] = a * lax.rsqrt(ms + RMS_EPS) * gain_ref[...]
    b_out[...] = jnp.dot(lhs, wb_ref[...], preferred_element_type=F32)


def _norm_proj(x, scale, shift, g, w, gain_row, seg128, tm, tn=512):
    b, t, _ = x.shape
    nt = MIX_WIDTH // tn
    per_b = pl.BlockSpec((None, 1, D_MODEL), lambda i, j, n: (i, 0, 0))
    out_blk = pl.BlockSpec((None, tm, tn), lambda i, j, n: (i, j, n))
    act = jax.ShapeDtypeStruct((b, t, MIX_WIDTH), F32)
    est = 2 * tm * D_MODEL * 4 + tm * D_MODEL * 2 + 4 * D_MODEL * tn * 2 + 10 * tm * tn * 4
    return pl.pallas_call(
        _norm_proj_kernel,
        grid=(b, t // tm, nt),
        in_specs=[pl.BlockSpec((None, tm, D_MODEL), lambda i, j, n: (i, j, 0)), per_b, per_b,
                  pl.BlockSpec((1, D_MODEL), lambda i, j, n: (0, 0)),
                  pl.BlockSpec((D_MODEL, tn), lambda i, j, n: (0, n)),
                  pl.BlockSpec((D_MODEL, tn), lambda i, j, n: (0, nt + n)),
                  pl.BlockSpec((1, tn), lambda i, j, n: (0, 0)),
                  pl.BlockSpec(seg128.shape, lambda i, j, n: (0, 0))],
        out_specs=[out_blk, out_blk],
        out_shape=[act, act],
        scratch_shapes=[pltpu.VMEM((tm, D_MODEL), BF16)],
        compiler_params=_params(("arbitrary", "arbitrary", "arbitrary"), est),
        name="norm_proj",
    )(x, scale, shift, g, w, w, gain_row, seg128)


def _attn_kernel(*refs, tq, offset, top_steps):
    q_ref, k_ref, v_ref = refs[:3]
    pk_ref, pv_ref = refs[3:5] if offset else (None, None)
    wat_ref, o_ref, kb_scr, vb_scr, z_scr, w_scr, acc_scr = refs[5 if offset else 3:]
    i = pl.program_id(2)

    @pl.when(i == 0)
    def _():
        for dst, past, new in ((kb_scr, pk_ref, k_ref), (vb_scr, pv_ref, v_ref)):
            end = offset + new.shape[0]
            if past is not None:
                dst[0:offset, :] = past[...].astype(BF16)
            dst[offset:end, :] = new[...].astype(BF16)
            if dst.shape[0] > end:
                dst[end:, :] = jnp.zeros((dst.shape[0] - end, SB_HEAD), BF16)

    q = (q_ref[...] * (SB_SCALE * LOG2_E)).astype(BF16)
    q_first = offset + i * tq
    first = q_first // KEY_STEP
    wat = wat_ref[...]

    def at_step(ref, ks):
        return ref[pl.ds(pl.multiple_of(ks * KEY_STEP, KEY_STEP), KEY_STEP), :]

    def logits(q_rows, ks):
        return lax.dot_general(q_rows, at_step(kb_scr, ks), (((1,), (1,)), ((), ())),
                               preferred_element_type=F32)

    def stick(z2, carry, valid):
        cost = jnp.maximum(z2, 0.0) + jnp.log2(1.0 + jnp.exp2(-jnp.abs(z2)))
        if valid is not None:
            cost = jnp.where(valid, cost, 0.0)
        tail_in = jnp.dot(_split_hi_lo(cost), wat, preferred_element_type=F32)
        weights = jnp.exp2(z2 - cost - (tail_in + carry))
        if valid is not None:
            weights = jnp.where(valid, weights, 0.0)
        return weights.astype(BF16), carry + (tail_in[:, 0:1] + cost[:, 0:1])

    acc_scr[...] = jnp.zeros_like(acc_scr)
    carry = None
    for j in reversed(range(top_steps)):
        lo = j * KEY_STEP
        rows = tq - lo
        z2 = logits(q[lo:], first + j)
        key_pos = (first + j) * KEY_STEP + lax.broadcasted_iota(jnp.int32, (rows, KEY_STEP), 1)
        q_pos = q_first + lo + lax.broadcasted_iota(jnp.int32, (rows, KEY_STEP), 0)
        seen = 0 if carry is None else carry.shape[0]
        prev = jnp.zeros((rows - seen, 1), F32)
        if carry is not None:
            prev = jnp.concatenate([prev, carry], axis=0)
        weights, carry = stick(z2, prev, key_pos < q_pos)
        if j > 0:
            acc_scr[lo:, :] += jnp.dot(weights, at_step(vb_scr, first + j),
                                       preferred_element_type=F32)
        else:
            w_scr[...] = weights

    z_scr[...] = logits(q, jnp.maximum(first - 1, 0))

    def staged(n, state):
        carry, ks_prev = state
        ks = first - 1 - n
        acc_scr[...] += jnp.dot(w_scr[...], at_step(vb_scr, ks_prev), preferred_element_type=F32)
        weights, carry = stick(z_scr[...], carry, None)
        w_scr[...] = weights
        z_scr[...] = logits(q, jnp.maximum(ks - 1, 0))
        return carry, ks

    _, ks_prev = lax.fori_loop(0, first, staged, (carry, first))
    o_ref[...] = acc_scr[...] + jnp.dot(w_scr[...], at_step(vb_scr, ks_prev),
                                        preferred_element_type=F32)


def _attn_weights():
    j = jnp.arange(2 * KEY_STEP) % KEY_STEP
    s = jnp.arange(KEY_STEP)
    return (j[:, None] > s[None, :]).astype(BF16)


def _sb_attention(q, k_new, v_new, past_k, past_v, wat, tq):
    b, t, _ = q.shape
    offset = 0 if past_k is None else past_k.shape[1]
    tk = -(-(offset + t) // KEY_STEP) * KEY_STEP
    top_steps = -(-tq // KEY_STEP)
    assert offset % KEY_STEP == 0 and (tq % KEY_STEP == 0 or t == tq)
    head_blk = lambda rows, idx: pl.BlockSpec((None, rows, SB_HEAD), idx)
    q_blk = head_blk(tq, lambda i, h, j: (i, j, h))
    whole = lambda rows: head_blk(rows, lambda i, h, j: (i, 0, h))
    operands = [q, k_new, v_new] + ([past_k, past_v] if offset else []) + [wat]
    in_specs = ([q_blk, whole(t), whole(t)] + ([whole(offset)] * 2 if offset else [])
                + [pl.BlockSpec(wat.shape, lambda i, h, j: (0, 0))])
    est = 2 * 2 * tk * SB_HEAD * 4 + 2 * tk * SB_HEAD * 2 + 24 * tq * KEY_STEP * 4
    return pl.pallas_call(
        functools.partial(_attn_kernel, tq=tq, offset=offset, top_steps=top_steps),
        grid=(b, SB_HEADS, t // tq),
        in_specs=in_specs,
        out_specs=q_blk,
        out_shape=jax.ShapeDtypeStruct(q.shape, F32),
        scratch_shapes=[pltpu.VMEM((tk, SB_HEAD), BF16), pltpu.VMEM((tk, SB_HEAD), BF16),
                        pltpu.VMEM((tq, KEY_STEP), F32), pltpu.VMEM((tq, KEY_STEP), BF16),
                        pltpu.VMEM((tq, SB_HEAD), F32)],
        compiler_params=_params(("arbitrary", "arbitrary", "arbitrary"), est),
        name="sb_attention",
    )(*operands)


def _gated_out_kernel(o_ref, z_ref, x_ref, gate_ref, w_ref, out_ref):
    z = z_ref[...]
    lhs = (o_ref[...] * (z * _sigmoid(z))).astype(BF16)
    out_ref[...] = x_ref[...] + gate_ref[...] * jnp.dot(lhs, w_ref[...],
                                                        preferred_element_type=F32)


def _gated_out(o, z, x, gate, w, tm):
    b, t, _ = x.shape
    act_blk = pl.BlockSpec((None, tm, MIX_WIDTH), lambda i, j: (i, j, 0))
    x_blk = pl.BlockSpec((None, tm, D_MODEL), lambda i, j: (i, j, 0))
    est = 2 * 2 * tm * MIX_WIDTH * 4 + 2 * MIX_WIDTH * D_MODEL * 2 + 4 * tm * MIX_WIDTH * 4
    return pl.pallas_call(
        _gated_out_kernel,
        grid=(b, t // tm),
        in_specs=[act_blk, act_blk, x_blk,
                  pl.BlockSpec((None, 1, D_MODEL), lambda i, j: (i, 0, 0)),
                  pl.BlockSpec((MIX_WIDTH, D_MODEL), lambda i, j: (0, 0))],
        out_specs=x_blk,
        out_shape=jax.ShapeDtypeStruct(x.shape, F32),
        compiler_params=_params(("arbitrary", "arbitrary"), est),
        name="gated_out",
    )(o, z, x, gate, w)


def _state_to_lanes(s):
    problems = s.shape[0] * s.shape[1]
    s = s.reshape(problems // CHUNK_LANES, PAIR_ROWS, 2, RWKV_HEAD, RWKV_HEAD)
    return s.transpose(0, 3, 4, 2, 1).reshape(-1, RWKV_HEAD, RWKV_HEAD, CHUNK_LANES)


def _lanes_to_state(s, b):
    s = s.reshape(-1, RWKV_HEAD, RWKV_HEAD, 2, PAIR_ROWS)
    return s.transpose(0, 4, 3, 1, 2).reshape(b, RWKV_HEADS, RWKV_HEAD, RWKV_HEAD)


def _trunk(x, ada_a, ada_b, past_k, past_v, shift0, wkv0, pa, pb, consts):
    b, t, _ = x.shape
    seg64, seg128, wat = consts
    split3 = lambda a: [v[:, None, :] for v in jnp.split(a, 3, axis=-1)]
    shift_a, scale_a, gate_a = split3(ada_a)
    shift_b, scale_b, gate_b = split3(ada_b)

    r, kkn, dec, nb, km, v, z, shift_new = _rwkv_pre(
        x, scale_a, shift_a, shift0[:, None, :], pa, seg64, tm=min(t, 512))
    tb = min(t, 16 if b * RWKV_HEADS <= CHUNK_LANES else 8)
    y, s_new = _rwkv_recurrence(kkn, dec, nb, km, r, v, _state_to_lanes(wkv0), tb=tb)
    x_mid = _rwkv_post(y, r, km, v, z, x, gate_a, pa, seg64, tm=min(t, 128))

    zeros = jnp.zeros((b, 1, D_MODEL), F32)
    k_new, v_new = _norm_proj(x_mid, zeros, zeros, pb['kv_norm_g'], pb['kv_w'], pb['k_gain'],
                              seg128, tm=min(t, 512))
    q, z1 = _norm_proj(x_mid, scale_b, shift_b, pb['norm_g'], pb['w_in'], pb['q_gain'],
                       seg128, tm=min(t, 512))
    flat = lambda c: None if c is None else c.reshape(b, c.shape[1], MIX_WIDTH)
    o = _sb_attention(q, k_new, v_new, flat(past_k), flat(past_v), wat, tq=min(t, 512))
    x_out = _gated_out(o, z1, x_mid, gate_b, pb['w_out'], tm=min(t, 256))

    heads = lambda a: a.reshape(b, t, SB_HEADS, SB_HEAD)
    return (x_out, heads(k_new), heads(v_new), _lanes_to_state(s_new, b)[None],
            shift_new.reshape(1, b, D_MODEL))


def kernel(x_prompt, x_sample, cache_k, cache_v, state_wkv, state_shift, c_prompt, c_sample, a_norm_g, a_ada_w, a_ada_b, a_w_in, a_mu_in, a_mu_w, a_mu_a, a_w0, a_w1, a_w2, a_a0, a_a1, a_a2, a_k_k, a_k_a, a_r_k, a_ln_g, a_ln_b, a_w_out, kv_norm_g, kv_w, k_gain, b_norm_g, b_ada_w, b_ada_b, b_w_in, b_q_gain, b_w_out):
    bp = x_prompt.shape[0]
    row = lambda a: a.reshape(1, -1)
    pa = dict(norm_g=a_norm_g, mu_in=a_mu_in[0], mu_w=a_mu_w, mu_a=a_mu_a,
              w_in=a_w_in[0].astype(BF16), w1=a_w1[0].astype(BF16), a1=a_a1[0].astype(BF16),
              w2=a_w2[0].astype(BF16), a2=a_a2[0].astype(BF16), w0=a_w0, a0=a_a0,
              k_k=a_k_k, k_a=a_k_a, r_k=row(a_r_k[0]), ln_g=a_ln_g, ln_b=a_ln_b,
              w_out=a_w_out[0].astype(BF16))
    heads_per_tile = 512 // SB_HEAD
    pb = dict(kv_norm_g=row(kv_norm_g), kv_w=kv_w.astype(BF16),
              k_gain=jnp.tile(row(k_gain), (1, heads_per_tile)),
              norm_g=b_norm_g, w_in=b_w_in[0].astype(BF16),
              q_gain=jnp.tile(b_q_gain, (1, heads_per_tile)),
              w_out=b_w_out[0].astype(BF16))
    consts = (_seg_weights(RWKV_HEAD), _seg_weights(SB_HEAD), _attn_weights())

    ada = _ada(jnp.concatenate([c_prompt, c_sample], axis=0),
               jnp.concatenate([a_ada_w, b_ada_w], axis=0),
               jnp.concatenate([a_ada_b, b_ada_b], axis=0)[:, None, :])
    zero_shift = jnp.zeros((bp, D_MODEL), F32)
    zero_wkv = jnp.zeros((bp,) + state_wkv.shape[2:], F32)
    y_p, k_p, v_p, wkv_p, shift_p = _trunk(
        x_prompt, ada[0, :bp], ada[1, :bp], None, None, zero_shift, zero_wkv, pa, pb, consts)
    y_s, k_s, v_s, wkv_s, shift_s = _trunk(
        x_sample, ada[0, bp:], ada[1, bp:], cache_k, cache_v, state_shift[0], state_wkv[0],
        pa, pb, consts)
    return (y_p, y_s, k_p, v_p, wkv_p, shift_p, k_s, v_s, wkv_s, shift_s)
```

```python
import functools

import jax
import jax.numpy as jnp
from jax import lax
from jax.experimental import pallas as pl
from jax.experimental.pallas import tpu as pltpu

F32 = jnp.float32
BF16 = jnp.bfloat16

D_MODEL = 1024
MIX_WIDTH = 2048
RWKV_HEAD = 64
RWKV_HEADS = MIX_WIDTH // RWKV_HEAD
SB_HEAD = 128
SB_HEADS = MIX_WIDTH // SB_HEAD
LORA_RANK = 64
RMS_EPS = 1e-6
GN_EPS = 64e-5
L2_EPS = 1e-12
SB_SCALE = SB_HEAD ** -0.5
LOG2_E = 1.4426950408889634

V7X_LANES = 128
V7X_SUBLANES = 8
V7X_MXU_WIDTH = 256
V7X_VMEM_BYTES = 64 * 1024 * 1024

SEG_TILE = V7X_MXU_WIDTH
PAIR_ROWS = V7X_LANES
CHUNK_LANES = 2 * PAIR_ROWS
KEY_STEP = V7X_MXU_WIDTH


def _vmem_limit(estimate_bytes):
    return int(min(V7X_VMEM_BYTES - 8 * 1024 * 1024, max(32 * 1024 * 1024, 2 * estimate_bytes)))


def _params(semantics, estimate_bytes):
    return pltpu.CompilerParams(dimension_semantics=semantics,
                                vmem_limit_bytes=_vmem_limit(estimate_bytes))


def _split_hi_lo(x):
    hi = x.astype(BF16)
    lo = (x - hi.astype(F32)).astype(BF16)
    return jnp.concatenate([hi, lo], axis=-1)


def _seg_sum(x, seg_w):
    m, c = x.shape
    tiles = c // SEG_TILE
    stacked = jnp.concatenate(
        [_split_hi_lo(x[:, j * SEG_TILE:(j + 1) * SEG_TILE]) for j in range(tiles)], axis=0)
    res = jnp.dot(stacked, seg_w, preferred_element_type=F32)
    return jnp.concatenate([res[j * m:(j + 1) * m] for j in range(tiles)], axis=1)


def _seg_weights(seg):
    j = jnp.arange(2 * SEG_TILE) % SEG_TILE
    l = jnp.arange(SEG_TILE)
    return ((j[:, None] // seg) == (l[None, :] // seg)).astype(BF16)


def _to_tiles(val, ref):
    m, c = val.shape
    grouped = val.reshape(m // V7X_SUBLANES, V7X_SUBLANES, c)
    for p in range(c // V7X_LANES):
        ref[:, p * V7X_SUBLANES:(p + 1) * V7X_SUBLANES, :] = grouped[:, :, p * V7X_LANES:(p + 1) * V7X_LANES]


def _from_tiles(ref):
    tiles = ref[...]
    groups, rows, _ = tiles.shape
    return jnp.concatenate(
        [tiles[:, p * V7X_SUBLANES:(p + 1) * V7X_SUBLANES, :].reshape(groups * V7X_SUBLANES, V7X_LANES)
         for p in range(rows // V7X_SUBLANES)], axis=1)


def _log_sigmoid(x):
    return jnp.minimum(x, 0.0) - jnp.log(1.0 + jnp.exp(-jnp.abs(x)))


def _sigmoid(x):
    return 1.0 / (1.0 + jnp.exp(-x))


def _rms_mod(x, g, scale, shift):
    ms = jnp.mean(x * x, axis=-1, keepdims=True)
    return (x * lax.rsqrt(ms + RMS_EPS)) * g * (1.0 + scale) + shift


def _ada_kernel(c_ref, w_ref, b_ref, o_ref):
    o_ref[...] = jnp.dot(c_ref[...], w_ref[...], precision=lax.Precision.HIGHEST,
                         preferred_element_type=F32) + b_ref[...]


def _ada(c_all, w, b):
    m = c_all.shape[0]
    layers = w.shape[0]
    return pl.pallas_call(
        _ada_kernel,
        grid=(layers, 3),
        in_specs=[pl.BlockSpec((m, D_MODEL), lambda l, n: (0, 0)),
                  pl.BlockSpec((None, D_MODEL, D_MODEL), lambda l, n: (l, 0, n)),
                  pl.BlockSpec((None, 1, D_MODEL), lambda l, n: (l, 0, n))],
        out_specs=pl.BlockSpec((None, m, D_MODEL), lambda l, n: (l, 0, n)),
        out_shape=jax.ShapeDtypeStruct((layers, m, 3 * D_MODEL), F32),
        compiler_params=_params(("arbitrary", "arbitrary"), 2 * 4 * D_MODEL * D_MODEL),
        name="ada",
    )(c_all, w, b)


def _pre_kernel(x_ref, sc_ref, sh_ref, g_ref, sp_ref, mu_ref, muw_ref, mua_ref, w1_ref, a1_ref,
                wr_ref, wk_ref, wv_ref, wz_ref, w2_ref, a2_ref, w0_ref, a0_ref, kk_ref, ka_ref,
                segw_ref,
                r_out, kkn_out, dec_out, nb_out, km_out, v_out, z_out, shift_out,
                xs_scr, lw_scr, la_scr, carry_scr):
    t = pl.program_id(1)
    n = pl.program_id(2)
    last_t = pl.num_programs(1) - 1

    @pl.when(n == 0)
    def _():
        @pl.when(t == 0)
        def _():
            carry_scr[...] = sp_ref[...]

        h = _rms_mod(x_ref[...], g_ref[...], sc_ref[...], sh_ref[...])
        tm = h.shape[0]
        row = lax.broadcasted_iota(jnp.int32, h.shape, 0)
        h_prev = jnp.where(row == 0, carry_scr[...], pltpu.roll(h, 1, axis=0))
        dx = h_prev - h
        for p in range(4):
            xs_scr[p] = (h + mu_ref[p:p + 1, :] * dx).astype(BF16)
        xw = (h + dx * muw_ref[...]).astype(BF16)
        lw_scr[...] = jnp.tanh(jnp.dot(xw, w1_ref[...], preferred_element_type=F32))
        xa = (h + dx * mua_ref[...]).astype(BF16)
        la_scr[...] = jnp.dot(xa, a1_ref[...], preferred_element_type=F32)
        carry_scr[...] = h[tm - 1:tm, :]

        @pl.when(t == last_t)
        def _():
            shift_out[...] = h[tm - 1:tm, :]

    r = jnp.dot(xs_scr[0], wr_ref[...], preferred_element_type=F32)
    k = jnp.dot(xs_scr[1], wk_ref[...], preferred_element_type=F32)
    v = jnp.dot(xs_scr[2], wv_ref[...], preferred_element_type=F32)
    z = jnp.dot(xs_scr[3], wz_ref[...], preferred_element_type=F32)
    wl = w0_ref[...] + jnp.dot(lw_scr[...].astype(BF16), w2_ref[...], preferred_element_type=F32)
    dec = jnp.exp(-jnp.exp(_log_sigmoid(wl) - 0.5))
    a = _sigmoid(a0_ref[...] + jnp.dot(la_scr[...].astype(BF16), a2_ref[...],
                                       preferred_element_type=F32))
    kk = k * kk_ref[...]
    kkn = kk * lax.rsqrt(_seg_sum(kk * kk, segw_ref[...]) + L2_EPS)
    _to_tiles(r, r_out)
    _to_tiles(kkn, kkn_out)
    _to_tiles(dec, dec_out)
    _to_tiles(-(kkn * a), nb_out)
    _to_tiles(k * (1.0 + (a - 1.0) * ka_ref[...]), km_out)
    _to_tiles(v, v_out)
    _to_tiles(z, z_out)


def _tiled_shape(t, b):
    return (t // V7X_SUBLANES, b * (MIX_WIDTH // V7X_LANES) * V7X_SUBLANES, V7X_LANES)


def _rwkv_pre(x, scale, shift, shift_prev, p, seg64, tm, tn=512):
    b, t, _ = x.shape
    nt = MIX_WIDTH // tn
    row = lambda w: pl.BlockSpec((1, w), lambda i, j, n: (0, 0))
    per_b = pl.BlockSpec((None, 1, D_MODEL), lambda i, j, n: (i, 0, 0))
    col = lambda rows: pl.BlockSpec((rows, tn), lambda i, j, n: (0, n))
    w_in_part = lambda part: pl.BlockSpec((D_MODEL, tn), lambda i, j, n: (0, part * nt + n))
    out_blk = pl.BlockSpec((tm // V7X_SUBLANES, tn // V7X_LANES * V7X_SUBLANES, V7X_LANES),
                           lambda i, j, n: (j, i * nt + n, 0))
    act = jax.ShapeDtypeStruct(_tiled_shape(t, b), F32)
    est = (2 * tm * D_MODEL * 4 + 4 * tm * D_MODEL * 2 + 2 * 4 * D_MODEL * tn * 2
           + 2 * 7 * tm * tn * 4 + 8 * tm * tn * 4)
    return pl.pallas_call(
        _pre_kernel,
        grid=(b, t // tm, nt),
        in_specs=[pl.BlockSpec((None, tm, D_MODEL), lambda i, j, n: (i, j, 0)),
                  per_b, per_b, row(D_MODEL), per_b,
                  pl.BlockSpec((4, D_MODEL), lambda i, j, n: (0, 0)), row(D_MODEL), row(D_MODEL),
                  pl.BlockSpec((D_MODEL, LORA_RANK), lambda i, j, n: (0, 0)),
                  pl.BlockSpec((D_MODEL, LORA_RANK), lambda i, j, n: (0, 0)),
                  w_in_part(0), w_in_part(1), w_in_part(2), w_in_part(3),
                  col(LORA_RANK), col(LORA_RANK), col(1), col(1), col(1), col(1),
                  pl.BlockSpec(seg64.shape, lambda i, j, n: (0, 0))],
        out_specs=[out_blk] * 7 + [per_b],
        out_shape=[act] * 7 + [jax.ShapeDtypeStruct((b, 1, D_MODEL), F32)],
        scratch_shapes=[pltpu.VMEM((4, tm, D_MODEL), BF16),
                        pltpu.VMEM((tm, LORA_RANK), F32),
                        pltpu.VMEM((tm, LORA_RANK), F32),
                        pltpu.VMEM((1, D_MODEL), F32)],
        compiler_params=_params(("arbitrary", "arbitrary", "arbitrary"), est),
        name="rwkv_pre",
    )(x, scale, shift, p['norm_g'], shift_prev, p['mu_in'], p['mu_w'], p['mu_a'], p['w1'], p['a1'],
      p['w_in'], p['w_in'], p['w_in'], p['w_in'], p['w2'], p['a2'], p['w0'], p['a0'],
      p['k_k'], p['k_a'], seg64)


def _rec_kernel(kk_ref, dec_ref, nb_ref, km_ref, r_ref, v_ref, s0_ref, y_out, s_scr,
                kk_t, dec_t, nb_t, km_t, r_t, v_t, d_scr, kk_s, nb_s, km_s, r_s, *, tb, chunks):
    step = pl.program_id(0)

    @pl.when(step == 0)
    def _():
        s_scr[...] = s0_ref[...]

    jobs = ((kk_ref, kk_t), (dec_ref, dec_t), (nb_ref, nb_t), (km_ref, km_t), (r_ref, r_t),
            (v_ref, v_t))
    row_groups = RWKV_HEAD // V7X_SUBLANES

    def pair_rows(i, c):
        start = c * PAIR_ROWS * V7X_SUBLANES + i % V7X_SUBLANES
        return i // V7X_SUBLANES, pl.ds(start, PAIR_ROWS, stride=V7X_SUBLANES)

    def to_lanes(job, i, c, slot):
        ref, dst = job
        group, rows = pair_rows(i, c)
        xt = ref[group, rows, :].T
        dst[slot, c] = jnp.concatenate([xt[:RWKV_HEAD], xt[RWKV_HEAD:]], axis=1)

    def key_sum(p):
        acc = p[0:V7X_SUBLANES]
        for j in range(1, row_groups):
            acc = acc + p[j * V7X_SUBLANES:(j + 1) * V7X_SUBLANES]
        for shift in (4, 2, 1):
            acc = acc + pltpu.roll(acc, shift, axis=0)
        return acc

    def rows64(x8):
        return jnp.concatenate([x8] * row_groups, axis=0)

    sub = lax.broadcasted_iota(jnp.int32, (V7X_SUBLANES, CHUNK_LANES), 0)

    for c in range(chunks):
        for job in jobs:
            to_lanes(job, 0, c, 0)
        d_scr[c] = jnp.ones((RWKV_HEAD, CHUNK_LANES), F32)

    def token_step(i, carry):
        slot = i % 2
        i_next = jnp.minimum(i + 1, tb - 1)
        for c in range(chunks):
            d_prev = d_scr[c]
            d_now = d_prev * dec_t[slot, c]
            d_inv = 1.0 / d_now
            d_scr[c] = d_now
            kk_s[c] = kk_t[slot, c] * d_prev
            nb_s[c] = nb_t[slot, c] * d_inv
            km_s[c] = km_t[slot, c] * d_inv
            r_s[c] = r_t[slot, c] * d_now
            y_rows = []
            for vo in range(row_groups):
                if vo < len(jobs):
                    to_lanes(jobs[vo], i_next, c, 1 - slot)
                v8 = v_t[slot, c, vo * V7X_SUBLANES:(vo + 1) * V7X_SUBLANES, :]
                y8 = jnp.zeros((V7X_SUBLANES, CHUNK_LANES), F32)
                for vi in range(V7X_SUBLANES):
                    row = vo * V7X_SUBLANES + vi
                    s_old = s_scr[c, row]
                    s_kk = rows64(key_sum(s_old * kk_s[c]))
                    v_b = jnp.broadcast_to(v8[vi:vi + 1], (RWKV_HEAD, CHUNK_LANES))
                    s_new = s_old + s_kk * nb_s[c] + v_b * km_s[c]
                    s_scr[c, row] = s_new
                    y8 = jnp.where(sub == vi, key_sum(s_new * r_s[c]), y8)
                y_rows.append(y8)
            y = jnp.concatenate(y_rows, axis=0)
            group, rows = pair_rows(i, c)
            y_out[group, rows, :] = jnp.concatenate(
                [y[:, :PAIR_ROWS], y[:, PAIR_ROWS:]], axis=0).T
        return carry

    lax.fori_loop(0, tb, token_step, 0)

    def fold_decay(row, carry):
        for c in range(chunks):
            s_scr[c, row] = s_scr[c, row] * d_scr[c]
        return carry

    lax.fori_loop(0, RWKV_HEAD, fold_decay, 0)


def _rwkv_recurrence(kkn, dec, nb, km, r, v, s0, tb):
    groups, rows, _ = kkn.shape
    t = groups * V7X_SUBLANES
    chunks = rows // (PAIR_ROWS * V7X_SUBLANES)
    act_blk = pl.BlockSpec((tb // V7X_SUBLANES, rows, V7X_LANES), lambda j: (j, 0, 0))
    state_blk = pl.BlockSpec(s0.shape, lambda j: (0, 0, 0, 0))
    tile = pltpu.VMEM((2, chunks, RWKV_HEAD, CHUNK_LANES), F32)
    state_bytes = s0.size * 4
    est = 2 * 7 * tb * rows * V7X_LANES * 4 // V7X_SUBLANES + 4 * state_bytes
    return pl.pallas_call(
        functools.partial(_rec_kernel, tb=tb, chunks=chunks),
        grid=(t // tb,),
        in_specs=[act_blk] * 6 + [state_blk],
        out_specs=[act_blk, state_blk],
        out_shape=[jax.ShapeDtypeStruct(kkn.shape, F32), jax.ShapeDtypeStruct(s0.shape, F32)],
        scratch_shapes=[tile] * 6 + [pltpu.VMEM((chunks, RWKV_HEAD, CHUNK_LANES), F32)] * 5,
        compiler_params=_params(("arbitrary",), est),
        name="rwkv_recurrence",
    )(kkn, dec, nb, km, r, v, s0)


def _post_kernel(y_ref, r_ref, km_ref, v_ref, z_ref, x_ref, gate_ref, lng_ref, lnb_ref, rk_ref,
                 wout_ref, segw_ref, o_ref):
    seg_w = segw_ref[...]
    inv = 1.0 / RWKV_HEAD
    y = _from_tiles(y_ref)
    d = y - _seg_sum(y, seg_w) * inv
    var = _seg_sum(d * d, seg_w) * inv
    yn = d * lax.rsqrt(var + GN_EPS) * lng_ref[...] + lnb_ref[...]
    bonus = _seg_sum(_from_tiles(r_ref) * _from_tiles(km_ref) * rk_ref[...], seg_w) * _from_tiles(v_ref)
    z = _from_tiles(z_ref)
    gated = ((yn + bonus) * (z * _sigmoid(z))).astype(BF16)
    o_ref[...] = x_ref[...] + gate_ref[...] * jnp.dot(gated, wout_ref[...],
                                                      preferred_element_type=F32)


def _rwkv_post(y, r, km, v, z, x, gate, p, seg64, tm):
    b, t, _ = x.shape
    act_blk = pl.BlockSpec((tm // V7X_SUBLANES, MIX_WIDTH // V7X_LANES * V7X_SUBLANES, V7X_LANES),
                           lambda i, j: (j, i, 0))
    x_blk = pl.BlockSpec((None, tm, D_MODEL), lambda i, j: (i, j, 0))
    row = pl.BlockSpec((1, MIX_WIDTH), lambda i, j: (0, 0))
    est = 2 * 5 * tm * MIX_WIDTH * 4 + 2 * MIX_WIDTH * D_MODEL * 2 + 12 * tm * MIX_WIDTH * 4
    return pl.pallas_call(
        _post_kernel,
        grid=(b, t // tm),
        in_specs=[act_blk] * 5 + [x_blk, pl.BlockSpec((None, 1, D_MODEL), lambda i, j: (i, 0, 0)),
                                  row, row, row,
                                  pl.BlockSpec((MIX_WIDTH, D_MODEL), lambda i, j: (0, 0)),
                                  pl.BlockSpec(seg64.shape, lambda i, j: (0, 0))],
        out_specs=x_blk,
        out_shape=jax.ShapeDtypeStruct(x.shape, F32),
        compiler_params=_params(("arbitrary", "arbitrary"), est),
        name="rwkv_post",
    )(y, r, km, v, z, x, gate, p['ln_g'], p['ln_b'], p['r_k'], p['w_out'], seg64)


def _norm_proj_kernel(x_ref, sc_ref, sh_ref, g_ref, wa_ref, wb_ref, gain_ref, segw_ref,
                      a_out, b_out, lhs_scr):
    @pl.when(pl.program_id(2) == 0)
    def _():
        lhs_scr[...] = _rms_mod(x_ref[...], g_ref[...], sc_ref[...], sh_ref[...]).astype(BF16)

    lhs = lhs_scr[...]
    a = jnp.dot(lhs, wa_ref[...], preferred_element_type=F32)
    ms = _seg_sum(a * a, segw_ref[...]) * (1.0 / SB_HEAD)
    a_out[...] = a * lax.rsqrt(ms + RMS_EPS) * gain_ref[...]
    b_out[...] = jnp.dot(lhs, wb_ref[...], preferred_element_type=F32)


def _norm_proj(x, scale, shift, g, w, gain_row, seg128, tm, tn=512):
    b, t, _ = x.shape
    nt = MIX_WIDTH // tn
    per_b = pl.BlockSpec((None, 1, D_MODEL), lambda i, j, n: (i, 0, 0))
    out_blk = pl.BlockSpec((None, tm, tn), lambda i, j, n: (i, j, n))
    act = jax.ShapeDtypeStruct((b, t, MIX_WIDTH), F32)
    est = 2 * tm * D_MODEL * 4 + tm * D_MODEL * 2 + 4 * D_MODEL * tn * 2 + 10 * tm * tn * 4
    return pl.pallas_call(
        _norm_proj_kernel,
        grid=(b, t // tm, nt),
        in_specs=[pl.BlockSpec((None, tm, D_MODEL), lambda i, j, n: (i, j, 0)), per_b, per_b,
                  pl.BlockSpec((1, D_MODEL), lambda i, j, n: (0, 0)),
                  pl.BlockSpec((D_MODEL, tn), lambda i, j, n: (0, n)),
                  pl.BlockSpec((D_MODEL, tn), lambda i, j, n: (0, nt + n)),
                  pl.BlockSpec((1, tn), lambda i, j, n: (0, 0)),
                  pl.BlockSpec(seg128.shape, lambda i, j, n: (0, 0))],
        out_specs=[out_blk, out_blk],
        out_shape=[act, act],
        scratch_shapes=[pltpu.VMEM((tm, D_MODEL), BF16)],
        compiler_params=_params(("arbitrary", "arbitrary", "arbitrary"), est),
        name="norm_proj",
    )(x, scale, shift, g, w, w, gain_row, seg128)


def _attn_kernel(*refs, tq, offset, top_steps):
    q_ref, k_ref, v_ref = refs[:3]
    pk_ref, pv_ref = refs[3:5] if offset else (None, None)
    wat_ref, o_ref, kb_scr, vb_scr, z_scr, w_scr, acc_scr = refs[5 if offset else 3:]
    i = pl.program_id(2)

    @pl.when(i == 0)
    def _():
        for dst, past, new in ((kb_scr, pk_ref, k_ref), (vb_scr, pv_ref, v_ref)):
            end = offset + new.shape[0]
            if past is not None:
                dst[0:offset, :] = past[...].astype(BF16)
            dst[offset:end, :] = new[...].astype(BF16)
            if dst.shape[0] > end:
                dst[end:, :] = jnp.zeros((dst.shape[0] - end, SB_HEAD), BF16)

    q = (q_ref[...] * (SB_SCALE * LOG2_E)).astype(BF16)
    q_first = offset + i * tq
    first = q_first // KEY_STEP
    wat = wat_ref[...]

    def at_step(ref, ks):
        return ref[pl.ds(pl.multiple_of(ks * KEY_STEP, KEY_STEP), KEY_STEP), :]

    def logits(q_rows, ks):
        return lax.dot_general(q_rows, at_step(kb_scr, ks), (((1,), (1,)), ((), ())),
                               preferred_element_type=F32)

    def stick(z2, carry, valid):
        cost = jnp.maximum(z2, 0.0) + jnp.log2(1.0 + jnp.exp2(jnp.minimum(z2, -z2)))
        if valid is not None:
            cost = jnp.where(valid, cost, 0.0)
        tail_in = jnp.dot(_split_hi_lo(cost), wat, preferred_element_type=F32)
        weights = jnp.exp2(z2 - cost - (tail_in + carry))
        if valid is not None:
            weights = jnp.where(valid, weights, 0.0)
        return weights.astype(BF16), carry + (tail_in[:, 0:1] + cost[:, 0:1])

    acc_scr[...] = jnp.zeros_like(acc_scr)
    carry = None
    for j in reversed(range(top_steps)):
        lo = j * KEY_STEP
        rows = tq - lo
        z2 = logits(q[lo:], first + j)
        key_pos = (first + j) * KEY_STEP + lax.broadcasted_iota(jnp.int32, (rows, KEY_STEP), 1)
        q_pos = q_first + lo + lax.broadcasted_iota(jnp.int32, (rows, KEY_STEP), 0)
        seen = 0 if carry is None else carry.shape[0]
        prev = jnp.zeros((rows - seen, 1), F32)
        if carry is not None:
            prev = jnp.concatenate([prev, carry], axis=0)
        weights, carry = stick(z2, prev, key_pos < q_pos)
        if j > 0:
            acc_scr[lo:, :] += jnp.dot(weights, at_step(vb_scr, first + j),
                                       preferred_element_type=F32)
        else:
            w_scr[...] = weights

    z_scr[...] = logits(q, jnp.maximum(first - 1, 0))

    def staged(n, state):
        carry, ks_prev = state
        ks = first - 1 - n
        acc_scr[...] += jnp.dot(w_scr[...], at_step(vb_scr, ks_prev), preferred_element_type=F32)
        weights, carry = stick(z_scr[...], carry, None)
        w_scr[...] = weights
        z_scr[...] = logits(q, jnp.maximum(ks - 1, 0))
        return carry, ks

    _, ks_prev = lax.fori_loop(0, first, staged, (carry, first))
    o_ref[...] = acc_scr[...] + jnp.dot(w_scr[...], at_step(vb_scr, ks_prev),
                                        preferred_element_type=F32)


def _attn_weights():
    j = jnp.arange(2 * KEY_STEP) % KEY_STEP
    s = jnp.arange(KEY_STEP)
    return (j[:, None] > s[None, :]).astype(BF16)


def _sb_attention(q, k_new, v_new, past_k, past_v, wat, tq):
    b, t, _ = q.shape
    offset = 0 if past_k is None else past_k.shape[1]
    tk = -(-(offset + t) // KEY_STEP) * KEY_STEP
    top_steps = -(-tq // KEY_STEP)
    assert offset % KEY_STEP == 0 and (tq % KEY_STEP == 0 or t == tq)
    head_blk = lambda rows, idx: pl.BlockSpec((None, rows, SB_HEAD), idx)
    q_blk = head_blk(tq, lambda i, h, j: (i, j, h))
    whole = lambda rows: head_blk(rows, lambda i, h, j: (i, 0, h))
    operands = [q, k_new, v_new] + ([past_k, past_v] if offset else []) + [wat]
    in_specs = ([q_blk, whole(t), whole(t)] + ([whole(offset)] * 2 if offset else [])
                + [pl.BlockSpec(wat.shape, lambda i, h, j: (0, 0))])
    est = 2 * 2 * tk * SB_HEAD * 4 + 2 * tk * SB_HEAD * 2 + 24 * tq * KEY_STEP * 4
    return pl.pallas_call(
        functools.partial(_attn_kernel, tq=tq, offset=offset, top_steps=top_steps),
        grid=(b, SB_HEADS, t // tq),
        in_specs=in_specs,
        out_specs=q_blk,
        out_shape=jax.ShapeDtypeStruct(q.shape, F32),
        scratch_shapes=[pltpu.VMEM((tk, SB_HEAD), BF16), pltpu.VMEM((tk, SB_HEAD), BF16),
                        pltpu.VMEM((tq, KEY_STEP), F32), pltpu.VMEM((tq, KEY_STEP), BF16),
                        pltpu.VMEM((tq, SB_HEAD), F32)],
        compiler_params=_params(("arbitrary", "arbitrary", "arbitrary"), est),
        name="sb_attention",
    )(*operands)


def _gated_out_kernel(o_ref, z_ref, x_ref, gate_ref, w_ref, out_ref):
    z = z_ref[...]
    lhs = (o_ref[...] * (z * _sigmoid(z))).astype(BF16)
    out_ref[...] = x_ref[...] + gate_ref[...] * jnp.dot(lhs, w_ref[...],
                                                        preferred_element_type=F32)


def _gated_out(o, z, x, gate, w, tm):
    b, t, _ = x.shape
    act_blk = pl.BlockSpec((None, tm, MIX_WIDTH), lambda i, j: (i, j, 0))
    x_blk = pl.BlockSpec((None, tm, D_MODEL), lambda i, j: (i, j, 0))
    est = 2 * 2 * tm * MIX_WIDTH * 4 + 2 * MIX_WIDTH * D_MODEL * 2 + 4 * tm * MIX_WIDTH * 4
    return pl.pallas_call(
        _gated_out_kernel,
        grid=(b, t // tm),
        in_specs=[act_blk, act_blk, x_blk,
                  pl.BlockSpec((None, 1, D_MODEL), lambda i, j: (i, 0, 0)),
                  pl.BlockSpec((MIX_WIDTH, D_MODEL), lambda i, j: (0, 0))],
        out_specs=x_blk,
        out_shape=jax.ShapeDtypeStruct(x.shape, F32),
        compiler_params=_params(("arbitrary", "arbitrary"), est),
        name="gated_out",
    )(o, z, x, gate, w)


def _state_to_lanes(s):
    problems = s.shape[0] * s.shape[1]
    s = s.reshape(problems // CHUNK_LANES, PAIR_ROWS, 2, RWKV_HEAD, RWKV_HEAD)
    return s.transpose(0, 3, 4, 2, 1).reshape(-1, RWKV_HEAD, RWKV_HEAD, CHUNK_LANES)


def _lanes_to_state(s, b):
    s = s.reshape(-1, RWKV_HEAD, RWKV_HEAD, 2, PAIR_ROWS)
    return s.transpose(0, 4, 3, 1, 2).reshape(b, RWKV_HEADS, RWKV_HEAD, RWKV_HEAD)


def _trunk(x, ada_a, ada_b, past_k, past_v, shift0, wkv0, pa, pb, consts):
    b, t, _ = x.shape
    seg64, seg128, wat = consts
    split3 = lambda a: [v[:, None, :] for v in jnp.split(a, 3, axis=-1)]
    shift_a, scale_a, gate_a = split3(ada_a)
    shift_b, scale_b, gate_b = split3(ada_b)

    r, kkn, dec, nb, km, v, z, shift_new = _rwkv_pre(
        x, scale_a, shift_a, shift0[:, None, :], pa, seg64, tm=min(t, 512))
    tb = min(t, 32 if b * RWKV_HEADS <= CHUNK_LANES else 8)
    y, s_new = _rwkv_recurrence(kkn, dec, nb, km, r, v, _state_to_lanes(wkv0), tb=tb)
    x_mid = _rwkv_post(y, r, km, v, z, x, gate_a, pa, seg64, tm=min(t, 128))

    zeros = jnp.zeros((b, 1, D_MODEL), F32)
    k_new, v_new = _norm_proj(x_mid, zeros, zeros, pb['kv_norm_g'], pb['kv_w'], pb['k_gain'],
                              seg128, tm=min(t, 512))
    q, z1 = _norm_proj(x_mid, scale_b, shift_b, pb['norm_g'], pb['w_in'], pb['q_gain'],
                       seg128, tm=min(t, 512))
    flat = lambda c: None if c is None else c.reshape(b, c.shape[1], MIX_WIDTH)
    o = _sb_attention(q, k_new, v_new, flat(past_k), flat(past_v), wat, tq=min(t, 1024))
    x_out = _gated_out(o, z1, x_mid, gate_b, pb['w_out'], tm=min(t, 256))

    heads = lambda a: a.reshape(b, t, SB_HEADS, SB_HEAD)
    return (x_out, heads(k_new), heads(v_new), _lanes_to_state(s_new, b)[None],
            shift_new.reshape(1, b, D_MODEL))


def kernel(x_prompt, x_sample, cache_k, cache_v, state_wkv, state_shift, c_prompt, c_sample, a_norm_g, a_ada_w, a_ada_b, a_w_in, a_mu_in, a_mu_w, a_mu_a, a_w0, a_w1, a_w2, a_a0, a_a1, a_a2, a_k_k, a_k_a, a_r_k, a_ln_g, a_ln_b, a_w_out, kv_norm_g, kv_w, k_gain, b_norm_g, b_ada_w, b_ada_b, b_w_in, b_q_gain, b_w_out):
    bp = x_prompt.shape[0]
    row = lambda a: a.reshape(1, -1)
    pa = dict(norm_g=a_norm_g, mu_in=a_mu_in[0], mu_w=a_mu_w, mu_a=a_mu_a,
              w_in=a_w_in[0].astype(BF16), w1=a_w1[0].astype(BF16), a1=a_a1[0].astype(BF16),
              w2=a_w2[0].astype(BF16), a2=a_a2[0].astype(BF16), w0=a_w0, a0=a_a0,
              k_k=a_k_k, k_a=a_k_a, r_k=row(a_r_k[0]), ln_g=a_ln_g, ln_b=a_ln_b,
              w_out=a_w_out[0].astype(BF16))
    heads_per_tile = 512 // SB_HEAD
    pb = dict(kv_norm_g=row(kv_norm_g), kv_w=kv_w.astype(BF16),
              k_gain=jnp.tile(row(k_gain), (1, heads_per_tile)),
              norm_g=b_norm_g, w_in=b_w_in[0].astype(BF16),
              q_gain=jnp.tile(b_q_gain, (1, heads_per_tile)),
              w_out=b_w_out[0].astype(BF16))
    consts = (_seg_weights(RWKV_HEAD), _seg_weights(SB_HEAD), _attn_weights())

    ada = _ada(jnp.concatenate([c_prompt, c_sample], axis=0),
               jnp.concatenate([a_ada_w, b_ada_w], axis=0),
               jnp.concatenate([a_ada_b, b_ada_b], axis=0)[:, None, :])
    zero_shift = jnp.zeros((bp, D_MODEL), F32)
    zero_wkv = jnp.zeros((bp,) + state_wkv.shape[2:], F32)
    y_p, k_p, v_p, wkv_p, shift_p = _trunk(
        x_prompt, ada[0, :bp], ada[1, :bp], None, None, zero_shift, zero_wkv, pa, pb, consts)
    y_s, k_s, v_s, wkv_s, shift_s = _trunk(
        x_sample, ada[0, bp:], ada[1, bp:], cache_k, cache_v, state_shift[0], state_wkv[0],
        pa, pb, consts)
    return (y_p, y_s, k_p, v_p, wkv_p, shift_p, k_s, v_s, wkv_s, shift_s)
```

```python
import functools

import jax
import jax.numpy as jnp
from jax import lax
from jax.experimental import pallas as pl
from jax.experimental.pallas import tpu as pltpu

F32 = jnp.float32
BF16 = jnp.bfloat16

D_MODEL = 1024
MIX_WIDTH = 2048
RWKV_HEAD = 64
RWKV_HEADS = MIX_WIDTH // RWKV_HEAD
SB_HEAD = 128
SB_HEADS = MIX_WIDTH // SB_HEAD
LORA_RANK = 64
RMS_EPS = 1e-6
GN_EPS = 64e-5
L2_EPS = 1e-12
SB_SCALE = SB_HEAD ** -0.5
LOG2_E = 1.4426950408889634

V7X_LANES = 128
V7X_SUBLANES = 8
V7X_MXU_WIDTH = 256
V7X_VMEM_BYTES = 64 * 1024 * 1024

SEG_TILE = V7X_MXU_WIDTH
PAIR_ROWS = V7X_LANES
CHUNK_LANES = 2 * PAIR_ROWS
KEY_STEP = V7X_MXU_WIDTH


def _vmem_limit(estimate_bytes):
    return int(min(V7X_VMEM_BYTES - 8 * 1024 * 1024, max(32 * 1024 * 1024, 2 * estimate_bytes)))


def _params(semantics, estimate_bytes):
    return pltpu.CompilerParams(dimension_semantics=semantics,
                                vmem_limit_bytes=_vmem_limit(estimate_bytes))


def _split_hi_lo(x):
    hi = x.astype(BF16)
    lo = (x - hi.astype(F32)).astype(BF16)
    return jnp.concatenate([hi, lo], axis=-1)


def _seg_sum(x, seg_w):
    m, c = x.shape
    tiles = c // SEG_TILE
    stacked = jnp.concatenate(
        [_split_hi_lo(x[:, j * SEG_TILE:(j + 1) * SEG_TILE]) for j in range(tiles)], axis=0)
    res = jnp.dot(stacked, seg_w, preferred_element_type=F32)
    return jnp.concatenate([res[j * m:(j + 1) * m] for j in range(tiles)], axis=1)


def _seg_weights(seg):
    j = jnp.arange(2 * SEG_TILE) % SEG_TILE
    l = jnp.arange(SEG_TILE)
    return ((j[:, None] // seg) == (l[None, :] // seg)).astype(BF16)


def _to_tiles(val, ref):
    m, c = val.shape
    grouped = val.reshape(m // V7X_SUBLANES, V7X_SUBLANES, c)
    for p in range(c // V7X_LANES):
        ref[:, p * V7X_SUBLANES:(p + 1) * V7X_SUBLANES, :] = grouped[:, :, p * V7X_LANES:(p + 1) * V7X_LANES]


def _from_tiles(ref):
    tiles = ref[...]
    groups, rows, _ = tiles.shape
    return jnp.concatenate(
        [tiles[:, p * V7X_SUBLANES:(p + 1) * V7X_SUBLANES, :].reshape(groups * V7X_SUBLANES, V7X_LANES)
         for p in range(rows // V7X_SUBLANES)], axis=1)


def _log_sigmoid(x):
    return jnp.minimum(x, 0.0) - jnp.log(1.0 + jnp.exp(-jnp.abs(x)))


def _sigmoid(x):
    return 1.0 / (1.0 + jnp.exp(-x))


def _rms_mod(x, g, scale, shift):
    ms = jnp.mean(x * x, axis=-1, keepdims=True)
    return (x * lax.rsqrt(ms + RMS_EPS)) * g * (1.0 + scale) + shift


def _ada_kernel(c_ref, w_ref, b_ref, o_ref):
    o_ref[...] = jnp.dot(c_ref[...], w_ref[...], precision=lax.Precision.HIGHEST,
                         preferred_element_type=F32) + b_ref[...]


def _ada(c_all, w, b):
    m = c_all.shape[0]
    layers = w.shape[0]
    return pl.pallas_call(
        _ada_kernel,
        grid=(layers, 3),
        in_specs=[pl.BlockSpec((m, D_MODEL), lambda l, n: (0, 0)),
                  pl.BlockSpec((None, D_MODEL, D_MODEL), lambda l, n: (l, 0, n)),
                  pl.BlockSpec((None, 1, D_MODEL), lambda l, n: (l, 0, n))],
        out_specs=pl.BlockSpec((None, m, D_MODEL), lambda l, n: (l, 0, n)),
        out_shape=jax.ShapeDtypeStruct((layers, m, 3 * D_MODEL), F32),
        compiler_params=_params(("arbitrary", "arbitrary"), 2 * 4 * D_MODEL * D_MODEL),
        name="ada",
    )(c_all, w, b)


def _pre_kernel(x_ref, sc_ref, sh_ref, g_ref, sp_ref, mu_ref, muw_ref, mua_ref, w1_ref, a1_ref,
                wr_ref, wk_ref, wv_ref, wz_ref, w2_ref, a2_ref, w0_ref, a0_ref, kk_ref, ka_ref,
                segw_ref,
                r_out, kkn_out, dec_out, nb_out, km_out, v_out, z_out, shift_out,
                xs_scr, lw_scr, la_scr, carry_scr):
    t = pl.program_id(1)
    n = pl.program_id(2)
    last_t = pl.num_programs(1) - 1

    @pl.when(n == 0)
    def _():
        @pl.when(t == 0)
        def _():
            carry_scr[...] = sp_ref[...]

        h = _rms_mod(x_ref[...], g_ref[...], sc_ref[...], sh_ref[...])
        tm = h.shape[0]
        row = lax.broadcasted_iota(jnp.int32, h.shape, 0)
        h_prev = jnp.where(row == 0, carry_scr[...], pltpu.roll(h, 1, axis=0))
        dx = h_prev - h
        for p in range(4):
            xs_scr[p] = (h + mu_ref[p:p + 1, :] * dx).astype(BF16)
        xw = (h + dx * muw_ref[...]).astype(BF16)
        lw_scr[...] = jnp.tanh(jnp.dot(xw, w1_ref[...], preferred_element_type=F32))
        xa = (h + dx * mua_ref[...]).astype(BF16)
        la_scr[...] = jnp.dot(xa, a1_ref[...], preferred_element_type=F32)
        carry_scr[...] = h[tm - 1:tm, :]

        @pl.when(t == last_t)
        def _():
            shift_out[...] = h[tm - 1:tm, :]

    r = jnp.dot(xs_scr[0], wr_ref[...], preferred_element_type=F32)
    k = jnp.dot(xs_scr[1], wk_ref[...], preferred_element_type=F32)
    v = jnp.dot(xs_scr[2], wv_ref[...], preferred_element_type=F32)
    z = jnp.dot(xs_scr[3], wz_ref[...], preferred_element_type=F32)
    wl = w0_ref[...] + jnp.dot(lw_scr[...].astype(BF16), w2_ref[...], preferred_element_type=F32)
    dec = jnp.exp(-jnp.exp(_log_sigmoid(wl) - 0.5))
    a = _sigmoid(a0_ref[...] + jnp.dot(la_scr[...].astype(BF16), a2_ref[...],
                                       preferred_element_type=F32))
    kk = k * kk_ref[...]
    kkn = kk * lax.rsqrt(_seg_sum(kk * kk, segw_ref[...]) + L2_EPS)
    _to_tiles(r, r_out)
    _to_tiles(kkn, kkn_out)
    _to_tiles(dec, dec_out)
    _to_tiles(-(kkn * a), nb_out)
    _to_tiles(k * (1.0 + (a - 1.0) * ka_ref[...]), km_out)
    _to_tiles(v, v_out)
    _to_tiles(z, z_out)


def _tiled_shape(t, b):
    return (t // V7X_SUBLANES, b * (MIX_WIDTH // V7X_LANES) * V7X_SUBLANES, V7X_LANES)


def _rwkv_pre(x, scale, shift, shift_prev, p, seg64, tm, tn=512):
    b, t, _ = x.shape
    nt = MIX_WIDTH // tn
    row = lambda w: pl.BlockSpec((1, w), lambda i, j, n: (0, 0))
    per_b = pl.BlockSpec((None, 1, D_MODEL), lambda i, j, n: (i, 0, 0))
    col = lambda rows: pl.BlockSpec((rows, tn), lambda i, j, n: (0, n))
    w_in_part = lambda part: pl.BlockSpec((D_MODEL, tn), lambda i, j, n: (0, part * nt + n))
    out_blk = pl.BlockSpec((tm // V7X_SUBLANES, tn // V7X_LANES * V7X_SUBLANES, V7X_LANES),
                           lambda i, j, n: (j, i * nt + n, 0))
    act = jax.ShapeDtypeStruct(_tiled_shape(t, b), F32)
    est = (2 * tm * D_MODEL * 4 + 4 * tm * D_MODEL * 2 + 2 * 4 * D_MODEL * tn * 2
           + 2 * 7 * tm * tn * 4 + 8 * tm * tn * 4)
    return pl.pallas_call(
        _pre_kernel,
        grid=(b, t // tm, nt),
        in_specs=[pl.BlockSpec((None, tm, D_MODEL), lambda i, j, n: (i, j, 0)),
                  per_b, per_b, row(D_MODEL), per_b,
                  pl.BlockSpec((4, D_MODEL), lambda i, j, n: (0, 0)), row(D_MODEL), row(D_MODEL),
                  pl.BlockSpec((D_MODEL, LORA_RANK), lambda i, j, n: (0, 0)),
                  pl.BlockSpec((D_MODEL, LORA_RANK), lambda i, j, n: (0, 0)),
                  w_in_part(0), w_in_part(1), w_in_part(2), w_in_part(3),
                  col(LORA_RANK), col(LORA_RANK), col(1), col(1), col(1), col(1),
                  pl.BlockSpec(seg64.shape, lambda i, j, n: (0, 0))],
        out_specs=[out_blk] * 7 + [per_b],
        out_shape=[act] * 7 + [jax.ShapeDtypeStruct((b, 1, D_MODEL), F32)],
        scratch_shapes=[pltpu.VMEM((4, tm, D_MODEL), BF16),
                        pltpu.VMEM((tm, LORA_RANK), F32),
                        pltpu.VMEM((tm, LORA_RANK), F32),
                        pltpu.VMEM((1, D_MODEL), F32)],
        compiler_params=_params(("arbitrary", "arbitrary", "arbitrary"), est),
        name="rwkv_pre",
    )(x, scale, shift, p['norm_g'], shift_prev, p['mu_in'], p['mu_w'], p['mu_a'], p['w1'], p['a1'],
      p['w_in'], p['w_in'], p['w_in'], p['w_in'], p['w2'], p['a2'], p['w0'], p['a0'],
      p['k_k'], p['k_a'], seg64)


def _rec_kernel(kk_ref, dec_ref, nb_ref, km_ref, r_ref, v_ref, s0_ref, y_out, s_scr,
                kk_t, dec_t, nb_t, km_t, r_t, v_t, d_scr, kk_s, nb_s, km_s, r_s, y_s, *, tb, chunks):
    step = pl.program_id(0)

    @pl.when(step == 0)
    def _():
        s_scr[...] = s0_ref[...]

    jobs = ((kk_ref, kk_t), (dec_ref, dec_t), (nb_ref, nb_t), (km_ref, km_t), (r_ref, r_t),
            (v_ref, v_t))
    row_groups = RWKV_HEAD // V7X_SUBLANES

    def pair_rows(i, c):
        start = c * PAIR_ROWS * V7X_SUBLANES + i % V7X_SUBLANES
        return i // V7X_SUBLANES, pl.ds(start, PAIR_ROWS, stride=V7X_SUBLANES)

    def to_lanes(job, i, c, slot):
        ref, dst = job
        group, rows = pair_rows(i, c)
        xt = ref[group, rows, :].T
        dst[slot, c] = jnp.concatenate([xt[:RWKV_HEAD], xt[RWKV_HEAD:]], axis=1)

    def key_sum(p):
        acc = p[0:V7X_SUBLANES]
        for j in range(1, row_groups):
            acc = acc + p[j * V7X_SUBLANES:(j + 1) * V7X_SUBLANES]
        for shift in (4, 2, 1):
            acc = acc + pltpu.roll(acc, shift, axis=0)
        return acc

    def rows64(x8):
        return jnp.concatenate([x8] * row_groups, axis=0)

    sub = lax.broadcasted_iota(jnp.int32, (V7X_SUBLANES, CHUNK_LANES), 0)

    def from_lanes(y, i, c):
        group, rows = pair_rows(i, c)
        y_out[group, rows, :] = jnp.concatenate([y[:, :PAIR_ROWS], y[:, PAIR_ROWS:]], axis=0).T

    for c in range(chunks):
        for job in jobs:
            to_lanes(job, 0, c, 0)
        d_scr[c] = jnp.ones((RWKV_HEAD, CHUNK_LANES), F32)
        y_s[c] = jnp.zeros((RWKV_HEAD, CHUNK_LANES), F32)

    def token_step(i, carry):
        slot = i % 2
        i_next = jnp.minimum(i + 1, tb - 1)
        for c in range(chunks):
            from_lanes(y_s[c], jnp.maximum(i - 1, 0), c)
            d_prev = d_scr[c]
            d_now = d_prev * dec_t[slot, c]
            d_inv = 1.0 / d_now
            d_scr[c] = d_now
            kk_s[c] = kk_t[slot, c] * d_prev
            nb_s[c] = nb_t[slot, c] * d_inv
            km_s[c] = km_t[slot, c] * d_inv
            r_s[c] = r_t[slot, c] * d_now
            y_rows = []
            for vo in range(row_groups):
                if vo < len(jobs):
                    to_lanes(jobs[vo], i_next, c, 1 - slot)
                v8 = v_t[slot, c, vo * V7X_SUBLANES:(vo + 1) * V7X_SUBLANES, :]
                y8 = jnp.zeros((V7X_SUBLANES, CHUNK_LANES), F32)
                for vi in range(V7X_SUBLANES):
                    row = vo * V7X_SUBLANES + vi
                    s_old = s_scr[c, row]
                    s_kk = rows64(key_sum(s_old * kk_s[c]))
                    v_b = jnp.broadcast_to(v8[vi:vi + 1], (RWKV_HEAD, CHUNK_LANES))
                    s_new = s_old + s_kk * nb_s[c] + v_b * km_s[c]
                    s_scr[c, row] = s_new
                    y8 = jnp.where(sub == vi, key_sum(s_new * r_s[c]), y8)
                y_rows.append(y8)
            y_s[c] = jnp.concatenate(y_rows, axis=0)
        return carry

    lax.fori_loop(0, tb, token_step, 0)
    for c in range(chunks):
        from_lanes(y_s[c], tb - 1, c)

    def fold_decay(row, carry):
        for c in range(chunks):
            s_scr[c, row] = s_scr[c, row] * d_scr[c]
        return carry

    lax.fori_loop(0, RWKV_HEAD, fold_decay, 0)


def _rwkv_recurrence(kkn, dec, nb, km, r, v, s0, tb):
    groups, rows, _ = kkn.shape
    t = groups * V7X_SUBLANES
    chunks = rows // (PAIR_ROWS * V7X_SUBLANES)
    act_blk = pl.BlockSpec((tb // V7X_SUBLANES, rows, V7X_LANES), lambda j: (j, 0, 0))
    state_blk = pl.BlockSpec(s0.shape, lambda j: (0, 0, 0, 0))
    tile = pltpu.VMEM((2, chunks, RWKV_HEAD, CHUNK_LANES), F32)
    state_bytes = s0.size * 4
    est = 2 * 7 * tb * rows * V7X_LANES * 4 // V7X_SUBLANES + 4 * state_bytes
    return pl.pallas_call(
        functools.partial(_rec_kernel, tb=tb, chunks=chunks),
        grid=(t // tb,),
        in_specs=[act_blk] * 6 + [state_blk],
        out_specs=[act_blk, state_blk],
        out_shape=[jax.ShapeDtypeStruct(kkn.shape, F32), jax.ShapeDtypeStruct(s0.shape, F32)],
        scratch_shapes=[tile] * 6 + [pltpu.VMEM((chunks, RWKV_HEAD, CHUNK_LANES), F32)] * 6,
        compiler_params=_params(("arbitrary",), est),
        name="rwkv_recurrence",
    )(kkn, dec, nb, km, r, v, s0)


def _post_kernel(y_ref, r_ref, km_ref, v_ref, z_ref, x_ref, gate_ref, lng_ref, lnb_ref, rk_ref,
                 wout_ref, segw_ref, o_ref):
    seg_w = segw_ref[...]
    inv = 1.0 / RWKV_HEAD
    y = _from_tiles(y_ref)
    d = y - _seg_sum(y, seg_w) * inv
    var = _seg_sum(d * d, seg_w) * inv
    yn = d * lax.rsqrt(var + GN_EPS) * lng_ref[...] + lnb_ref[...]
    bonus = _seg_sum(_from_tiles(r_ref) * _from_tiles(km_ref) * rk_ref[...], seg_w) * _from_tiles(v_ref)
    z = _from_tiles(z_ref)
    gated = ((yn + bonus) * (z * _sigmoid(z))).astype(BF16)
    o_ref[...] = x_ref[...] + gate_ref[...] * jnp.dot(gated, wout_ref[...],
                                                      preferred_element_type=F32)


def _rwkv_post(y, r, km, v, z, x, gate, p, seg64, tm):
    b, t, _ = x.shape
    act_blk = pl.BlockSpec((tm // V7X_SUBLANES, MIX_WIDTH // V7X_LANES * V7X_SUBLANES, V7X_LANES),
                           lambda i, j: (j, i, 0))
    x_blk = pl.BlockSpec((None, tm, D_MODEL), lambda i, j: (i, j, 0))
    row = pl.BlockSpec((1, MIX_WIDTH), lambda i, j: (0, 0))
    est = 2 * 5 * tm * MIX_WIDTH * 4 + 2 * MIX_WIDTH * D_MODEL * 2 + 12 * tm * MIX_WIDTH * 4
    return pl.pallas_call(
        _post_kernel,
        grid=(b, t // tm),
        in_specs=[act_blk] * 5 + [x_blk, pl.BlockSpec((None, 1, D_MODEL), lambda i, j: (i, 0, 0)),
                                  row, row, row,
                                  pl.BlockSpec((MIX_WIDTH, D_MODEL), lambda i, j: (0, 0)),
                                  pl.BlockSpec(seg64.shape, lambda i, j: (0, 0))],
        out_specs=x_blk,
        out_shape=jax.ShapeDtypeStruct(x.shape, F32),
        compiler_params=_params(("arbitrary", "arbitrary"), est),
        name="rwkv_post",
    )(y, r, km, v, z, x, gate, p['ln_g'], p['ln_b'], p['r_k'], p['w_out'], seg64)


def _norm_proj_kernel(x_ref, sc_ref, sh_ref, g_ref, wa_ref, wb_ref, gain_ref, segw_ref,
                      a_out, b_out, lhs_scr):
    @pl.when(pl.program_id(2) == 0)
    def _():
        lhs_scr[...] = _rms_mod(x_ref[...], g_ref[...], sc_ref[...], sh_ref[...]).astype(BF16)

    lhs = lhs_scr[...]
    a = jnp.dot(lhs, wa_ref[...], preferred_element_type=F32)
    ms = _seg_sum(a * a, segw_ref[...]) * (1.0 / SB_HEAD)
    a_out[...] = a * lax.rsqrt(ms + RMS_EPS) * gain_ref[...]
    b_out[...] = jnp.dot(lhs, wb_ref[...], preferred_element_type=F32)


def _norm_proj(x, scale, shift, g, w, gain_row, seg128, tm, tn=512):
    b, t, _ = x.shape
    nt = MIX_WIDTH // tn
    per_b = pl.BlockSpec((None, 1, D_MODEL), lambda i, j, n: (i, 0, 0))
    out_blk = pl.BlockSpec((None, tm, tn), lambda i, j, n: (i, j, n))
    act = jax.ShapeDtypeStruct((b, t, MIX_WIDTH), F32)
    est = 2 * tm * D_MODEL * 4 + tm * D_MODEL * 2 + 4 * D_MODEL * tn * 2 + 10 * tm * tn * 4
    return pl.pallas_call(
        _norm_proj_kernel,
        grid=(b, t // tm, nt),
        in_specs=[pl.BlockSpec((None, tm, D_MODEL), lambda i, j, n: (i, j, 0)), per_b, per_b,
                  pl.BlockSpec((1, D_MODEL), lambda i, j, n: (0, 0)),
                  pl.BlockSpec((D_MODEL, tn), lambda i, j, n: (0, n)),
                  pl.BlockSpec((D_MODEL, tn), lambda i, j, n: (0, nt + n)),
                  pl.BlockSpec((1, tn), lambda i, j, n: (0, 0)),
                  pl.BlockSpec(seg128.shape, lambda i, j, n: (0, 0))],
        out_specs=[out_blk, out_blk],
        out_shape=[act, act],
        scratch_shapes=[pltpu.VMEM((tm, D_MODEL), BF16)],
        compiler_params=_params(("arbitrary", "arbitrary", "arbitrary"), est),
        name="norm_proj",
    )(x, scale, shift, g, w, w, gain_row, seg128)


def _attn_kernel(*refs, tq, offset, top_steps):
    q_ref, k_ref, v_ref = refs[:3]
    pk_ref, pv_ref = refs[3:5] if offset else (None, None)
    wat_ref, o_ref, kb_scr, vb_scr, z_scr, w_scr, acc_scr = refs[5 if offset else 3:]
    i = pl.program_id(2)

    @pl.when(i == 0)
    def _():
        for dst, past, new in ((kb_scr, pk_ref, k_ref), (vb_scr, pv_ref, v_ref)):
            end = offset + new.shape[0]
            if past is not None:
                dst[0:offset, :] = past[...].astype(BF16)
            dst[offset:end, :] = new[...].astype(BF16)
            if dst.shape[0] > end:
                dst[end:, :] = jnp.zeros((dst.shape[0] - end, SB_HEAD), BF16)

    q = (q_ref[...] * (SB_SCALE * LOG2_E)).astype(BF16)
    q_first = offset + i * tq
    first = q_first // KEY_STEP
    wat = wat_ref[...]

    def at_step(ref, ks):
        return ref[pl.ds(pl.multiple_of(ks * KEY_STEP, KEY_STEP), KEY_STEP), :]

    def logits(q_rows, ks):
        return lax.dot_general(q_rows, at_step(kb_scr, ks), (((1,), (1,)), ((), ())),
                               preferred_element_type=F32)

    def stick(z2, carry, valid):
        cost = jnp.maximum(z2, 0.0) + jnp.log2(1.0 + jnp.exp2(jnp.minimum(z2, -z2)))
        if valid is not None:
            cost = jnp.where(valid, cost, 0.0)
        tail_in = jnp.dot(_split_hi_lo(cost), wat, preferred_element_type=F32)
        weights = jnp.exp2(z2 - cost - (tail_in + carry))
        if valid is not None:
            weights = jnp.where(valid, weights, 0.0)
        return weights.astype(BF16), carry + (tail_in[:, 0:1] + cost[:, 0:1])

    acc_scr[...] = jnp.zeros_like(acc_scr)
    carry = None
    for j in reversed(range(top_steps)):
        lo = j * KEY_STEP
        rows = tq - lo
        z2 = logits(q[lo:], first + j)
        key_pos = (first + j) * KEY_STEP + lax.broadcasted_iota(jnp.int32, (rows, KEY_STEP), 1)
        q_pos = q_first + lo + lax.broadcasted_iota(jnp.int32, (rows, KEY_STEP), 0)
        seen = 0 if carry is None else carry.shape[0]
        prev = jnp.zeros((rows - seen, 1), F32)
        if carry is not None:
            prev = jnp.concatenate([prev, carry], axis=0)
        weights, carry = stick(z2, prev, key_pos < q_pos)
        if j > 0:
            acc_scr[lo:, :] += jnp.dot(weights, at_step(vb_scr, first + j),
                                       preferred_element_type=F32)
        else:
            w_scr[...] = weights

    z_scr[...] = logits(q, jnp.maximum(first - 1, 0))

    def staged(n, state):
        carry, ks_prev = state
        ks = first - 1 - n
        acc_scr[...] += jnp.dot(w_scr[...], at_step(vb_scr, ks_prev), preferred_element_type=F32)
        weights, carry = stick(z_scr[...], carry, None)
        w_scr[...] = weights
        z_scr[...] = logits(q, jnp.maximum(ks - 1, 0))
        return carry, ks

    _, ks_prev = lax.fori_loop(0, first, staged, (carry, first))
    o_ref[...] = acc_scr[...] + jnp.dot(w_scr[...], at_step(vb_scr, ks_prev),
                                        preferred_element_type=F32)


def _attn_weights():
    j = jnp.arange(2 * KEY_STEP) % KEY_STEP
    s = jnp.arange(KEY_STEP)
    return (j[:, None] > s[None, :]).astype(BF16)


def _sb_attention(q, k_new, v_new, past_k, past_v, wat, tq):
    b, t, _ = q.shape
    offset = 0 if past_k is None else past_k.shape[1]
    tk = -(-(offset + t) // KEY_STEP) * KEY_STEP
    top_steps = -(-tq // KEY_STEP)
    assert offset % KEY_STEP == 0 and (tq % KEY_STEP == 0 or t == tq)
    head_blk = lambda rows, idx: pl.BlockSpec((None, rows, SB_HEAD), idx)
    q_blk = head_blk(tq, lambda i, h, j: (i, j, h))
    whole = lambda rows: head_blk(rows, lambda i, h, j: (i, 0, h))
    operands = [q, k_new, v_new] + ([past_k, past_v] if offset else []) + [wat]
    in_specs = ([q_blk, whole(t), whole(t)] + ([whole(offset)] * 2 if offset else [])
                + [pl.BlockSpec(wat.shape, lambda i, h, j: (0, 0))])
    est = 2 * 2 * tk * SB_HEAD * 4 + 2 * tk * SB_HEAD * 2 + 24 * tq * KEY_STEP * 4
    return pl.pallas_call(
        functools.partial(_attn_kernel, tq=tq, offset=offset, top_steps=top_steps),
        grid=(b, SB_HEADS, t // tq),
        in_specs=in_specs,
        out_specs=q_blk,
        out_shape=jax.ShapeDtypeStruct(q.shape, F32),
        scratch_shapes=[pltpu.VMEM((tk, SB_HEAD), BF16), pltpu.VMEM((tk, SB_HEAD), BF16),
                        pltpu.VMEM((tq, KEY_STEP), F32), pltpu.VMEM((tq, KEY_STEP), BF16),
                        pltpu.VMEM((tq, SB_HEAD), F32)],
        compiler_params=_params(("arbitrary", "arbitrary", "arbitrary"), est),
        name="sb_attention",
    )(*operands)


def _gated_out_kernel(o_ref, z_ref, x_ref, gate_ref, w_ref, out_ref):
    z = z_ref[...]
    lhs = (o_ref[...] * (z * _sigmoid(z))).astype(BF16)
    out_ref[...] = x_ref[...] + gate_ref[...] * jnp.dot(lhs, w_ref[...],
                                                        preferred_element_type=F32)


def _gated_out(o, z, x, gate, w, tm):
    b, t, _ = x.shape
    act_blk = pl.BlockSpec((None, tm, MIX_WIDTH), lambda i, j: (i, j, 0))
    x_blk = pl.BlockSpec((None, tm, D_MODEL), lambda i, j: (i, j, 0))
    est = 2 * 2 * tm * MIX_WIDTH * 4 + 2 * MIX_WIDTH * D_MODEL * 2 + 4 * tm * MIX_WIDTH * 4
    return pl.pallas_call(
        _gated_out_kernel,
        grid=(b, t // tm),
        in_specs=[act_blk, act_blk, x_blk,
                  pl.BlockSpec((None, 1, D_MODEL), lambda i, j: (i, 0, 0)),
                  pl.BlockSpec((MIX_WIDTH, D_MODEL), lambda i, j: (0, 0))],
        out_specs=x_blk,
        out_shape=jax.ShapeDtypeStruct(x.shape, F32),
        compiler_params=_params(("arbitrary", "arbitrary"), est),
        name="gated_out",
    )(o, z, x, gate, w)


def _state_to_lanes(s):
    problems = s.shape[0] * s.shape[1]
    s = s.reshape(problems // CHUNK_LANES, PAIR_ROWS, 2, RWKV_HEAD, RWKV_HEAD)
    return s.transpose(0, 3, 4, 2, 1).reshape(-1, RWKV_HEAD, RWKV_HEAD, CHUNK_LANES)


def _lanes_to_state(s, b):
    s = s.reshape(-1, RWKV_HEAD, RWKV_HEAD, 2, PAIR_ROWS)
    return s.transpose(0, 4, 3, 1, 2).reshape(b, RWKV_HEADS, RWKV_HEAD, RWKV_HEAD)


def _trunk(x, ada_a, ada_b, past_k, past_v, shift0, wkv0, pa, pb, consts):
    b, t, _ = x.shape
    seg64, seg128, wat = consts
    split3 = lambda a: [v[:, None, :] for v in jnp.split(a, 3, axis=-1)]
    shift_a, scale_a, gate_a = split3(ada_a)
    shift_b, scale_b, gate_b = split3(ada_b)

    r, kkn, dec, nb, km, v, z, shift_new = _rwkv_pre(
        x, scale_a, shift_a, shift0[:, None, :], pa, seg64, tm=min(t, 512))
    tb = min(t, 32 if b * RWKV_HEADS <= CHUNK_LANES else 8)
    y, s_new = _rwkv_recurrence(kkn, dec, nb, km, r, v, _state_to_lanes(wkv0), tb=tb)
    x_mid = _rwkv_post(y, r, km, v, z, x, gate_a, pa, seg64, tm=min(t, 128))

    zeros = jnp.zeros((b, 1, D_MODEL), F32)
    k_new, v_new = _norm_proj(x_mid, zeros, zeros, pb['kv_norm_g'], pb['kv_w'], pb['k_gain'],
                              seg128, tm=min(t, 512))
    q, z1 = _norm_proj(x_mid, scale_b, shift_b, pb['norm_g'], pb['w_in'], pb['q_gain'],
                       seg128, tm=min(t, 512))
    flat = lambda c: None if c is None else c.reshape(b, c.shape[1], MIX_WIDTH)
    o = _sb_attention(q, k_new, v_new, flat(past_k), flat(past_v), wat, tq=min(t, 1024))
    x_out = _gated_out(o, z1, x_mid, gate_b, pb['w_out'], tm=min(t, 256))

    heads = lambda a: a.reshape(b, t, SB_HEADS, SB_HEAD)
    return (x_out, heads(k_new), heads(v_new), _lanes_to_state(s_new, b)[None],
            shift_new.reshape(1, b, D_MODEL))


def kernel(x_prompt, x_sample, cache_k, cache_v, state_wkv, state_shift, c_prompt, c_sample, a_norm_g, a_ada_w, a_ada_b, a_w_in, a_mu_in, a_mu_w, a_mu_a, a_w0, a_w1, a_w2, a_a0, a_a1, a_a2, a_k_k, a_k_a, a_r_k, a_ln_g, a_ln_b, a_w_out, kv_norm_g, kv_w, k_gain, b_norm_g, b_ada_w, b_ada_b, b_w_in, b_q_gain, b_w_out):
    bp = x_prompt.shape[0]
    row = lambda a: a.reshape(1, -1)
    pa = dict(norm_g=a_norm_g, mu_in=a_mu_in[0], mu_w=a_mu_w, mu_a=a_mu_a,
              w_in=a_w_in[0].astype(BF16), w1=a_w1[0].astype(BF16), a1=a_a1[0].astype(BF16),
              w2=a_w2[0].astype(BF16), a2=a_a2[0].astype(BF16), w0=a_w0, a0=a_a0,
              k_k=a_k_k, k_a=a_k_a, r_k=row(a_r_k[0]), ln_g=a_ln_g, ln_b=a_ln_b,
              w_out=a_w_out[0].astype(BF16))
    heads_per_tile = 512 // SB_HEAD
    pb = dict(kv_norm_g=row(kv_norm_g), kv_w=kv_w.astype(BF16),
              k_gain=jnp.tile(row(k_gain), (1, heads_per_tile)),
              norm_g=b_norm_g, w_in=b_w_in[0].astype(BF16),
              q_gain=jnp.tile(b_q_gain, (1, heads_per_tile)),
              w_out=b_w_out[0].astype(BF16))
    consts = (_seg_weights(RWKV_HEAD), _seg_weights(SB_HEAD), _attn_weights())

    ada = _ada(jnp.concatenate([c_prompt, c_sample], axis=0),
               jnp.concatenate([a_ada_w, b_ada_w], axis=0),
               jnp.concatenate([a_ada_b, b_ada_b], axis=0)[:, None, :])
    zero_shift = jnp.zeros((bp, D_MODEL), F32)
    zero_wkv = jnp.zeros((bp,) + state_wkv.shape[2:], F32)
    y_p, k_p, v_p, wkv_p, shift_p = _trunk(
        x_prompt, ada[0, :bp], ada[1, :bp], None, None, zero_shift, zero_wkv, pa, pb, consts)
    y_s, k_s, v_s, wkv_s, shift_s = _trunk(
        x_sample, ada[0, bp:], ada[1, bp:], cache_k, cache_v, state_shift[0], state_wkv[0],
        pa, pb, consts)
    return (y_p, y_s, k_p, v_p, wkv_p, shift_p, k_s, v_s, wkv_s, shift_s)
```
